```python
import jax, jax.numpy as jnp
from jax import lax
import numpy as np

D_MODEL = 1024
BATCH = 4
SEQ = 8192
DEPTH = 1

MEM_LEN = 256
MIX_WIDTH = D_MODEL
HGRN_HEADS = 4
HGRN_WIDTH = MIX_WIDTH // 2
HGRN_HEAD_DIM = HGRN_WIDTH // HGRN_HEADS
HGRN_CHUNK = 32
MOBA_HEADS = 4
MOBA_WIDTH = MIX_WIDTH - HGRN_WIDTH
MOBA_HEAD_DIM = MOBA_WIDTH // MOBA_HEADS
MOBA_BLOCK = 256
MOBA_TOP_K = 3
MOBA_QUERY_CHUNK = 64
ROPE_THETA = 500000.0
ROT_DIM = MOBA_HEAD_DIM // 4
XATTN_HEADS = 4
XATTN_HEAD_DIM = D_MODEL // XATTN_HEADS
D_FF = 4 * D_MODEL
NORM_EPS = 1e-6
IN_SPLITS = (HGRN_WIDTH, 2 * HGRN_WIDTH, 3 * HGRN_WIDTH, 4 * HGRN_WIDTH,
             4 * HGRN_WIDTH + MOBA_WIDTH, 4 * HGRN_WIDTH + 2 * MOBA_WIDTH)
IN_COLS = 4 * HGRN_WIDTH + 3 * MOBA_WIDTH

kernel_name = "hymba_hgrn2_moba_xattn_layer"


def rms_norm(x, g):
    xf = x.astype(jnp.float32)
    y = xf * lax.rsqrt(jnp.mean(xf * xf, axis=-1, keepdims=True) + NORM_EPS)
    return (y * g.astype(jnp.float32)).astype(x.dtype)


def split_heads(t, n_heads):
    B, S, W = t.shape
    return t.reshape(B, S, n_heads, W // n_heads).transpose(0, 2, 1, 3)


def merge_heads(t):
    B, H, S, hd = t.shape
    return t.transpose(0, 2, 1, 3).reshape(B, S, H * hd)


def partial_rotary(t, positions):
    half = ROT_DIM // 2
    inv_freq = ROPE_THETA ** (-jnp.arange(half, dtype=jnp.float32) * 2.0 / ROT_DIM)
    ang = positions.astype(jnp.float32)[:, None] * inv_freq[None, :]
    cos, sin = jnp.cos(ang), jnp.sin(ang)
    tf = t.astype(jnp.float32)
    x1, x2 = tf[..., :half], tf[..., half:ROT_DIM]
    out = jnp.concatenate([x1 * cos - x2 * sin, x2 * cos + x1 * sin, tf[..., ROT_DIM:]], axis=-1)
    return out.astype(t.dtype)


def hgrn2_mixer(q, f_logit, inp, gate, lb, norm_g):
    B, S, _ = q.shape
    H, dh, C = HGRN_HEADS, HGRN_HEAD_DIM, HGRN_CHUNK
    n = S // C
    f = lb.astype(jnp.float32) + (1.0 - lb.astype(jnp.float32)) * jax.nn.sigmoid(f_logit.astype(jnp.float32))
    log_f = jnp.log(f)
    k = 1.0 - f

    def chunks(t):
        return split_heads(t.astype(jnp.float32), H).reshape(B, H, n, C, dh)

    qc, kc, vc, gc = chunks(q), chunks(k), chunks(inp), chunks(log_f)
    b = jnp.cumsum(gc, axis=3)
    b_last = b[..., -1:, :]
    q_t = qc * jnp.exp(b)
    k_t = kc * jnp.exp(-b)
    k_end = kc * jnp.exp(b_last - b)
    causal = jnp.tril(jnp.ones((C, C), dtype=bool))
    A = jnp.where(causal, jnp.einsum('bhncd,bhnsd->bhncs', q_t, k_t), 0.0)
    o_intra = jnp.einsum('bhncs,bhnsv->bhncv', A, vc)
    decay = jnp.exp(b_last[..., 0, :])

    def step(state, xs):
        q_n, k_n, v_n, dec_n = xs
        o_n = jnp.einsum('bhcd,bhdv->bhcv', q_n, state)
        new_state = dec_n[..., None] * state + jnp.einsum('bhsd,bhsv->bhdv', k_n, v_n)
        return new_state, o_n

    xs = (jnp.moveaxis(q_t, 2, 0), jnp.moveaxis(k_end, 2, 0),
          jnp.moveaxis(vc, 2, 0), jnp.moveaxis(decay, 2, 0))
    _, o_inter = lax.scan(step, jnp.zeros((B, H, dh, dh), jnp.float32), xs)
    o = (o_intra + jnp.moveaxis(o_inter, 0, 2)).reshape(B, H, S, dh)
    o = o * lax.rsqrt(jnp.mean(o * o, axis=-1, keepdims=True) + NORM_EPS)
    o = merge_heads(o) * norm_g.astype(jnp.float32) * jax.nn.silu(gate.astype(jnp.float32))
    return o.astype(q.dtype)


def moba_attention(q, k, v):
    B, H, S, hd = q.shape
    QC, BLK = MOBA_QUERY_CHUNK, MOBA_BLOCK
    n_blocks = -(-S // BLK)
    pad = n_blocks * BLK - S
    kb = jnp.pad(k, ((0, 0), (0, 0), (0, pad), (0, 0))).reshape(B, H, n_blocks, BLK, hd)
    vb = jnp.pad(v, ((0, 0), (0, 0), (0, pad), (0, 0))).reshape(B, H, n_blocks, BLK, hd)
    k_mean = jnp.mean(kb.astype(jnp.float32), axis=3)
    n_sel = min(MOBA_TOP_K, n_blocks)
    n_chunks = S // QC
    scale = hd ** -0.5
    b_idx = jnp.arange(B)[:, None, None, None]
    h_idx = jnp.arange(H)[None, :, None, None]
    block_ids = jnp.arange(n_blocks)
    slot_ids = jnp.arange(n_sel)
    q_chunks = jnp.moveaxis(q.reshape(B, H, n_chunks, QC, hd), 2, 0)

    def attend_chunk(args):
        qc, c = args
        start = c * QC
        blk = start // BLK
        q_pos = start + jnp.arange(QC)
        gate = jnp.einsum('bhqd,bhnd->bhqn', qc.astype(jnp.float32), k_mean)
        gate = jnp.where(block_ids < blk, gate, -jnp.inf)
        _, sel = lax.top_k(gate, n_sel)
        k_sel = kb[b_idx, h_idx, sel]
        v_sel = vb[b_idx, h_idx, sel]
        k_own = lax.dynamic_index_in_dim(kb, blk, axis=2, keepdims=False)
        v_own = lax.dynamic_index_in_dim(vb, blk, axis=2, keepdims=False)
        s_sel = jnp.einsum('bhqd,bhqnkd->bhqnk', qc, k_sel,
                           preferred_element_type=jnp.float32) * scale
        s_sel = jnp.where((slot_ids < blk)[:, None], s_sel, -jnp.inf)
        s_own = jnp.einsum('bhqd,bhkd->bhqk', qc, k_own,
                           preferred_element_type=jnp.float32) * scale
        k_pos = blk * BLK + jnp.arange(BLK)
        s_own = jnp.where(k_pos[None, :] <= q_pos[:, None], s_own, -jnp.inf)
        scores = jnp.concatenate([s_sel.reshape(B, H, QC, n_sel * BLK), s_own], axis=-1)
        p = jax.nn.softmax(scores, axis=-1).astype(v.dtype)
        p_sel = p[..., :n_sel * BLK].reshape(B, H, QC, n_sel, BLK)
        p_own = p[..., n_sel * BLK:]
        return (jnp.einsum('bhqnk,bhqnkd->bhqd', p_sel, v_sel)
                + jnp.einsum('bhqk,bhkd->bhqd', p_own, v_own))

    out = lax.map(attend_chunk, (q_chunks, jnp.arange(n_chunks)))
    return jnp.moveaxis(out, 0, 2).reshape(B, H, S, hd)


def cross_attention(h, mem_n, w_q, w_k, w_v, w_o):
    B, S, _ = h.shape
    q = split_heads(h @ w_q, XATTN_HEADS)
    k = split_heads(mem_n @ w_k, XATTN_HEADS)
    v = split_heads(mem_n @ w_v, XATTN_HEADS)
    s = jnp.einsum('bhsd,bhmd->bhsm', q, k, preferred_element_type=jnp.float32) * XATTN_HEAD_DIM ** -0.5
    p = jax.nn.softmax(s, axis=-1).astype(v.dtype)
    o = jnp.einsum('bhsm,bhmd->bhsd', p, v)
    return merge_heads(o) @ w_o


def setup_inputs(seed: int = 0) -> dict:
    key = jax.random.key(seed)
    ks = jax.random.split(key, 20)

    def normal(k, shape, scale):
        return jax.random.normal(k, shape, jnp.float32) * scale

    def gain(k, shape):
        return 1.0 + 0.05 * jax.random.normal(k, shape, jnp.float32)

    return {
        "x": normal(ks[0], (BATCH, SEQ, D_MODEL), 1.0),
        "mem": normal(ks[1], (BATCH, MEM_LEN, D_MODEL), 1.0),
        "norm_mix": gain(ks[2], (DEPTH, D_MODEL)),
        "w_in": normal(ks[3], (DEPTH, D_MODEL, IN_COLS), D_MODEL ** -0.5),
        "lb_logits": gain(ks[4], (DEPTH + 1, HGRN_WIDTH)),
        "hgrn_norm": gain(ks[5], (DEPTH, HGRN_WIDTH)),
        "w_out": normal(ks[6], (DEPTH, MIX_WIDTH, D_MODEL), MIX_WIDTH ** -0.5),
        "norm_xattn": gain(ks[7], (DEPTH, D_MODEL)),
        "norm_mem": gain(ks[8], (DEPTH, D_MODEL)),
        "w_xq": normal(ks[9], (DEPTH, D_MODEL, D_MODEL), D_MODEL ** -0.5),
        "w_xk": normal(ks[10], (DEPTH, D_MODEL, D_MODEL), D_MODEL ** -0.5),
        "w_xv": normal(ks[11], (DEPTH, D_MODEL, D_MODEL), D_MODEL ** -0.5),
        "w_xo": normal(ks[12], (DEPTH, D_MODEL, D_MODEL), D_MODEL ** -0.5),
        "norm_mlp": gain(ks[13], (DEPTH, D_MODEL)),
        "w_ff1": normal(ks[14], (DEPTH, D_MODEL, D_FF), D_MODEL ** -0.5),
        "w_ff2": normal(ks[15], (DEPTH, D_FF, D_MODEL), D_FF ** -0.5),
        "norm_final": gain(ks[16], (D_MODEL,)),
    }


def reference(x, mem, norm_mix, w_in, lb_logits, hgrn_norm, w_out, norm_xattn, norm_mem,
              w_xq, w_xk, w_xv, w_xo, norm_mlp, w_ff1, w_ff2, norm_final):
    S = x.shape[1]
    positions = jnp.arange(S, dtype=jnp.int32)
    lb_table = jnp.cumsum(jax.nn.softmax(lb_logits.astype(jnp.float32), axis=0), axis=0)
    for l in range(DEPTH):
        h = rms_norm(x, norm_mix[l])
        proj = h @ w_in[l]
        hq, hf, hi, hg, mq, mk, mv = jnp.split(proj, IN_SPLITS, axis=-1)
        o_hgrn = hgrn2_mixer(hq, hf, hi, hg, lb_table[l], hgrn_norm[l])
        mq = partial_rotary(split_heads(mq, MOBA_HEADS), positions)
        mk = partial_rotary(split_heads(mk, MOBA_HEADS), positions)
        mv = split_heads(mv, MOBA_HEADS)
        o_moba = merge_heads(moba_attention(mq, mk, mv))
        mixed = jnp.concatenate([o_hgrn, o_moba.astype(o_hgrn.dtype)], axis=-1)
        x = x + mixed @ w_out[l]
        h = rms_norm(x, norm_xattn[l])
        mem_n = rms_norm(mem, norm_mem[l])
        x = x + cross_attention(h, mem_n, w_xq[l], w_xk[l], w_xv[l], w_xo[l])
        h = rms_norm(x, norm_mlp[l])
        x = x + jnp.square(jax.nn.relu(h @ w_ff1[l])) @ w_ff2[l]
    return rms_norm(x, norm_final)
```

```python
import functools
import math

import jax
import jax.numpy as jnp
from jax import lax
from jax.experimental import pallas as pl
from jax.experimental.pallas import tpu as pltpu

F32 = jnp.float32
BF16 = jnp.bfloat16

D_MODEL = 1024
MEM_LEN = 256
HGRN_HEADS = 4
HGRN_WIDTH = 512
HGRN_HEAD_DIM = 128
HGRN_CHUNK = 32
MOBA_HEADS = 4
MOBA_WIDTH = 512
MOBA_HEAD_DIM = 128
MOBA_BLOCK = 256
MOBA_TOP_K = 3
ROPE_THETA = 500000.0
ROT_DIM = 32
XATTN_HEADS = 4
XATTN_HEAD_DIM = 256
D_FF = 4096
NORM_EPS = 1e-6
IN_COLS = 4 * HGRN_WIDTH + 3 * MOBA_WIDTH

LOG2E = math.log2(math.e)
NEG_BIG = -1e30

VMEM_LIMIT_BYTES = 56 * 1024 * 1024

HGRN_TILE = 256
TOKEN_TILE = 512


def _rms(x, g):
    return x * lax.rsqrt(jnp.mean(x * x, axis=-1, keepdims=True) + NORM_EPS) * g


def _dot(a, b):
    return jnp.dot(a, b, preferred_element_type=F32)


def _dot_nt(a, b):
    return lax.dot_general(a, b, (((1,), (1,)), ((), ())), preferred_element_type=F32)


def _const_spec(shape):
    zeros = (0,) * len(shape)
    return pl.BlockSpec(shape, lambda *_: zeros)


def _inproj_kernel(x_ref, g_ref, w_ref, cos_ref, sa_ref, sb_ref, ph_ref, pm_ref):
    h = _rms(x_ref[...], g_ref[...]).astype(BF16)
    hg_cols = 4 * HGRN_WIDTH
    ph_ref[...] = _dot(h, w_ref[:, :hg_cols])
    cos, sa, sb = cos_ref[...], sa_ref[...], sb_ref[...]
    q_scale = MOBA_HEAD_DIM ** -0.5 * LOG2E
    for part in range(2):
        for hd in range(MOBA_HEADS):
            lo = hg_cols + part * MOBA_WIDTH + hd * MOBA_HEAD_DIM
            t = _dot(h, w_ref[:, lo:lo + MOBA_HEAD_DIM])
            r = (t * cos + pltpu.roll(t, MOBA_HEAD_DIM - ROT_DIM // 2, 1) * sa
                 + pltpu.roll(t, ROT_DIM // 2, 1) * sb)
            if part == 0:
                r = r * q_scale
            o = part * MOBA_WIDTH + hd * MOBA_HEAD_DIM
            pm_ref[:, o:o + MOBA_HEAD_DIM] = r.astype(BF16)
    lo = hg_cols + 2 * MOBA_WIDTH
    pm_ref[:, 2 * MOBA_WIDTH:] = _dot(h, w_ref[:, lo:]).astype(BF16)


def _inproj(xf, g, w, cos_t, sa_t, sb_t, seq):
    n = xf.shape[0]
    tm = TOKEN_TILE
    tiles_per_seq = seq // tm
    return pl.pallas_call(
        _inproj_kernel,
        grid=(n // tm,),
        in_specs=[
            pl.BlockSpec((tm, D_MODEL), lambda i: (i, 0)),
            _const_spec((1, D_MODEL)),
            _const_spec((D_MODEL, IN_COLS)),
            pl.BlockSpec((tm, MOBA_HEAD_DIM), lambda i: (i % tiles_per_seq, 0)),
            pl.BlockSpec((tm, MOBA_HEAD_DIM), lambda i: (i % tiles_per_seq, 0)),
            pl.BlockSpec((tm, MOBA_HEAD_DIM), lambda i: (i % tiles_per_seq, 0)),
        ],
        out_specs=[
            pl.BlockSpec((tm, 4 * HGRN_WIDTH), lambda i: (i, 0)),
            pl.BlockSpec((tm, 3 * MOBA_WIDTH), lambda i: (i, 0)),
        ],
        out_shape=[
            jax.ShapeDtypeStruct((n, 4 * HGRN_WIDTH), F32),
            jax.ShapeDtypeStruct((n, 3 * MOBA_WIDTH), BF16),
        ],
        compiler_params=pltpu.CompilerParams(
            dimension_semantics=("arbitrary",), vmem_limit_bytes=VMEM_LIMIT_BYTES),
        name="inproj",
    )(xf, g, w, cos_t, sa_t, sb_t)


def _hgrn_kernel(ph_ref, lb_ref, ng_ref, cum_ref, o_ref,
                 st_ref, qt_ref, kt_ref, ke_ref, dec_ref, oi_ref):
    T, W, C, dh = HGRN_TILE, HGRN_WIDTH, HGRN_CHUNK, HGRN_HEAD_DIM
    n_chunks = T // C

    @pl.when(pl.program_id(1) == 0)
    def _():
        st_ref[...] = jnp.zeros_like(st_ref)

    hq = ph_ref[:, 0:W]
    hf = ph_ref[:, W:2 * W]
    lb = lb_ref[...]
    f = lb + (1.0 - lb) * jax.nn.sigmoid(hf)
    logf = jnp.log(f)
    kk = 1.0 - f

    l1 = logf.astype(BF16)
    r1 = logf - l1.astype(F32)
    l2 = r1.astype(BF16)
    l3 = (r1 - l2.astype(F32)).astype(BF16)
    cum = cum_ref[...]
    bb = _dot(cum, l1) + _dot(cum, l2) + _dot(cum, l3)
    b = bb[:T]
    b_last = bb[T:]

    qt_ref[...] = (hq * jnp.exp(b)).astype(BF16)
    kt_ref[...] = (kk * jnp.exp(-b)).astype(BF16)
    ke_ref[...] = (kk * jnp.exp(b_last - b)).astype(BF16)
    dec_ref[...] = jnp.exp(b_last)

    row = lax.broadcasted_iota(jnp.int32, (T, T), 0)
    col = lax.broadcasted_iota(jnp.int32, (T, T), 1)
    intra_mask = (row // C == col // C) & (col <= row)
    chunk_of_lane = lax.broadcasted_iota(jnp.int32, (dh, T), 1) // C

    for h in range(HGRN_HEADS):
        hs = slice(h * dh, (h + 1) * dh)
        vf = ph_ref[:, 2 * W + h * dh:2 * W + (h + 1) * dh]
        a = jnp.where(intra_mask, _dot_nt(qt_ref[:, hs], kt_ref[:, hs]), 0.0)
        oi_ref[...] = _dot(a.astype(BF16), vf.astype(BF16))
        v_t = vf.T
        ke = ke_ref[:, hs]
        st = st_ref[h]
        for c in range(n_chunks):
            cs = slice(c * C, (c + 1) * C)
            oi_ref[cs, :] += _dot_nt(qt_ref[cs, hs], st.astype(BF16))
            v_tc = jnp.where(chunk_of_lane == c, v_t, 0.0).astype(BF16)
            upd = _dot(v_tc, ke)
            st = st * dec_ref[c * C:c * C + 1, hs] + upd
        st_ref[h] = st
        o_h = oi_ref[...]
        o_h = o_h * lax.rsqrt(jnp.mean(o_h * o_h, axis=-1, keepdims=True) + NORM_EPS)
        g = ph_ref[:, 3 * W + h * dh:3 * W + (h + 1) * dh]
        o_h = o_h * ng_ref[:, hs] * (g * jax.nn.sigmoid(g))
        o_ref[:, hs] = o_h.astype(o_ref.dtype)


def _hgrn(ph, lb, ng, cum, batch, seq):
    n = ph.shape[0]
    T = HGRN_TILE
    steps = seq // T
    return pl.pallas_call(
        _hgrn_kernel,
        grid=(batch, steps),
        in_specs=[
            pl.BlockSpec((T, 4 * HGRN_WIDTH), lambda b, i: (b * steps + i, 0)),
            _const_spec((1, HGRN_WIDTH)),
            _const_spec((1, HGRN_WIDTH)),
            _const_spec((2 * T, T)),
        ],
        out_specs=pl.BlockSpec((T, HGRN_WIDTH), lambda b, i: (b * steps + i, 0)),
        out_shape=jax.ShapeDtypeStruct((n, HGRN_WIDTH), BF16),
        scratch_shapes=[
            pltpu.VMEM((HGRN_HEADS, HGRN_HEAD_DIM, HGRN_HEAD_DIM), F32),
            pltpu.VMEM((T, HGRN_WIDTH), BF16),
            pltpu.VMEM((T, HGRN_WIDTH), BF16),
            pltpu.VMEM((T, HGRN_WIDTH), BF16),
            pltpu.VMEM((T, HGRN_WIDTH), F32),
            pltpu.VMEM((T, HGRN_HEAD_DIM), F32),
        ],
        compiler_params=pltpu.CompilerParams(
            dimension_semantics=("arbitrary", "arbitrary"), vmem_limit_bytes=VMEM_LIMIT_BYTES),
        name="hgrn",
    )(ph, lb, ng, cum)


def _moba_kernel(q_ref, k_ref, vt_ref, o_ref, kmean_ref, sel_ref):
    BLK = MOBA_BLOCK
    n_blocks = k_ref.shape[0] // BLK
    i = pl.program_id(2)

    @pl.when(i == 0)
    def _():
        for j in range(n_blocks):
            kb = k_ref[j * BLK:(j + 1) * BLK, :].astype(F32)
            kmean_ref[j:j + 1, :] = jnp.sum(kb, axis=0, keepdims=True) * (1.0 / BLK)

    q = q_ref[...]

    gate = _dot_nt(kmean_ref[...].astype(BF16), q)
    blk_id = lax.broadcasted_iota(jnp.int32, gate.shape, 0)
    past = blk_id < i
    g = jnp.where(past, gate, -jnp.inf)
    chosen = jnp.zeros(gate.shape, F32)
    for _ in range(MOBA_TOP_K):
        m = jnp.max(g, axis=0, keepdims=True)
        idx = jnp.min(jnp.where(g == m, blk_id, n_blocks), axis=0, keepdims=True)
        hit = blk_id == idx
        chosen = jnp.where(hit, 1.0, chosen)
        g = jnp.where(hit, -jnp.inf, g)
    sel_ref[...] = jnp.where(past, chosen, 0.0)

    k_pos = lax.broadcasted_iota(jnp.int32, (BLK, BLK), 0)
    q_pos = lax.broadcasted_iota(jnp.int32, (BLK, BLK), 1)
    k_own = k_ref[pl.ds(pl.multiple_of(i * BLK, BLK), BLK), :]
    s = jnp.where(k_pos <= q_pos, _dot_nt(k_own, q), NEG_BIG)
    m0 = jnp.max(s, axis=0, keepdims=True)
    p = jnp.exp2(s - m0)
    l0 = jnp.sum(p, axis=0, keepdims=True)
    acc0 = _dot(vt_ref[i], p.astype(BF16))

    def body(j, carry):
        m, l, acc = carry
        kj = k_ref[pl.ds(pl.multiple_of(j * BLK, BLK), BLK), :]
        s = _dot_nt(kj, q)
        picked = sel_ref[pl.ds(j, 1), :] > 0.0
        m_new = jnp.maximum(m, jnp.where(picked, jnp.max(s, axis=0, keepdims=True), NEG_BIG))
        alpha = jnp.exp2(m - m_new)
        p = jnp.exp2(s - m_new)
        l = alpha * l + jnp.where(picked, jnp.sum(p, axis=0, keepdims=True), 0.0)
        pv = _dot(vt_ref[j], p.astype(BF16))
        acc = alpha * acc + jnp.where(picked, pv, 0.0)
        return m_new, l, acc

    _, l, acc = lax.fori_loop(0, i, body, (m0, l0, acc0))
    o_ref[...] = (acc / l).T.astype(o_ref.dtype)


def _moba(pm, vt, batch, seq):
    n = pm.shape[0]
    BLK, hd, H = MOBA_BLOCK, MOBA_HEAD_DIM, MOBA_HEADS
    n_blocks = seq // BLK
    return pl.pallas_call(
        _moba_kernel,
        grid=(batch, H, n_blocks),
        in_specs=[
            pl.BlockSpec((BLK, hd), lambda b, h, i: (b * n_blocks + i, h)),
            pl.BlockSpec((seq, hd), lambda b, h, i: (b, H + h)),
            pl.BlockSpec((None, None, n_blocks, hd, BLK), lambda b, h, i: (b, h, 0, 0, 0)),
        ],
        out_specs=pl.BlockSpec((BLK, hd), lambda b, h, i: (b * n_blocks + i, h)),
        out_shape=jax.ShapeDtypeStruct((n, MOBA_WIDTH), BF16),
        scratch_shapes=[pltpu.VMEM((n_blocks, hd), F32), pltpu.VMEM((n_blocks, BLK), F32)],
        compiler_params=pltpu.CompilerParams(
            dimension_semantics=("arbitrary", "arbitrary", "arbitrary"),
            vmem_limit_bytes=VMEM_LIMIT_BYTES),
        name="moba",
    )(pm, pm, vt)


def _memkv_kernel(mem_ref, g_ref, wk_ref, wv_ref, k_ref, v_ref):
    h = _rms(mem_ref[...], g_ref[...]).astype(BF16)
    k_ref[...] = _dot(h, wk_ref[...]).astype(BF16)
    v_ref[...] = _dot(h, wv_ref[...]).astype(BF16)


def _memkv(memf, g, wk, wv):
    n = memf.shape[0]
    tm = MEM_LEN
    return pl.pallas_call(
        _memkv_kernel,
        grid=(n // tm,),
        in_specs=[
            pl.BlockSpec((tm, D_MODEL), lambda i: (i, 0)),
            _const_spec((1, D_MODEL)),
            _const_spec((D_MODEL, D_MODEL)),
            _const_spec((D_MODEL, D_MODEL)),
        ],
        out_specs=[pl.BlockSpec((tm, D_MODEL), lambda i: (i, 0))] * 2,
        out_shape=[jax.ShapeDtypeStruct((n, D_MODEL), BF16)] * 2,
        compiler_params=pltpu.CompilerParams(
            dimension_semantics=("arbitrary",), vmem_limit_bytes=VMEM_LIMIT_BYTES),
        name="memkv",
    )(memf, g, wk, wv)


def _mix_xattn_kernel(x_ref, oh_ref, om_ref, wo_ref, g_ref, wq_ref, k_ref, v_ref, wxo_ref, y_ref):
    x1 = (x_ref[...] + _dot(oh_ref[...], wo_ref[:HGRN_WIDTH, :])
          + _dot(om_ref[...], wo_ref[HGRN_WIDTH:, :]))
    h = _rms(x1, g_ref[...]).astype(BF16)
    q = (_dot(h, wq_ref[...]) * (XATTN_HEAD_DIM ** -0.5 * LOG2E)).astype(BF16)
    outs = []
    for hd in range(XATTN_HEADS):
        hs = slice(hd * XATTN_HEAD_DIM, (hd + 1) * XATTN_HEAD_DIM)
        s = _dot_nt(q[:, hs], k_ref[:, hs])
        p = jnp.exp2(s - jnp.max(s, axis=-1, keepdims=True))
        l = jnp.sum(p, axis=-1, keepdims=True)
        outs.append((_dot(p.astype(BF16), v_ref[:, hs]) / l).astype(BF16))
    o = jnp.concatenate(outs, axis=-1)
    y_ref[...] = x1 + _dot(o, wxo_ref[...])


def _mix_xattn(xf, oh, om, wo, g, wq, kmem, vmem, wxo, batch, seq):
    n = xf.shape[0]
    tm = TOKEN_TILE
    steps = seq // tm
    tok = lambda w: pl.BlockSpec((tm, w), lambda b, i: (b * steps + i, 0))
    return pl.pallas_call(
        _mix_xattn_kernel,
        grid=(batch, steps),
        in_specs=[
            tok(D_MODEL), tok(HGRN_WIDTH), tok(MOBA_WIDTH),
            _const_spec((D_MODEL, D_MODEL)),
            _const_spec((1, D_MODEL)),
            _const_spec((D_MODEL, D_MODEL)),
            pl.BlockSpec((MEM_LEN, D_MODEL), lambda b, i: (b, 0)),
            pl.BlockSpec((MEM_LEN, D_MODEL), lambda b, i: (b, 0)),
            _const_spec((D_MODEL, D_MODEL)),
        ],
        out_specs=tok(D_MODEL),
        out_shape=jax.ShapeDtypeStruct((n, D_MODEL), F32),
        compiler_params=pltpu.CompilerParams(
            dimension_semantics=("arbitrary", "arbitrary"), vmem_limit_bytes=VMEM_LIMIT_BYTES),
        name="mix_xattn",
    )(xf, oh, om, wo, g, wq, kmem, vmem, wxo)


def _mlp_kernel(x_ref, g_ref, w1_ref, w2_ref, gf_ref, y_ref, *, final_norm):
    x = x_ref[...]
    h = _rms(x, g_ref[...]).astype(BF16)
    ff_chunk = D_MODEL
    acc = x
    for c in range(D_FF // ff_chunk):
        cs = slice(c * ff_chunk, (c + 1) * ff_chunk)
        z = jnp.maximum(_dot(h, w1_ref[:, cs]), 0.0)
        acc = acc + _dot((z * z).astype(BF16), w2_ref[cs, :])
    y_ref[...] = _rms(acc, gf_ref[...]) if final_norm else acc


def _mlp(xf, g, w1, w2, gf, final_norm):
    n = xf.shape[0]
    tm = TOKEN_TILE
    return pl.pallas_call(
        functools.partial(_mlp_kernel, final_norm=final_norm),
        grid=(n // tm,),
        in_specs=[
            pl.BlockSpec((tm, D_MODEL), lambda i: (i, 0)),
            _const_spec((1, D_MODEL)),
            _const_spec((D_MODEL, D_FF)),
            _const_spec((D_FF, D_MODEL)),
            _const_spec((1, D_MODEL)),
        ],
        out_specs=pl.BlockSpec((tm, D_MODEL), lambda i: (i, 0)),
        out_shape=jax.ShapeDtypeStruct((n, D_MODEL), F32),
        compiler_params=pltpu.CompilerParams(
            dimension_semantics=("arbitrary",), vmem_limit_bytes=VMEM_LIMIT_BYTES),
        name="mlp",
    )(xf, g, w1, w2, gf)


def _rotary_tables(seq):
    half = ROT_DIM // 2
    inv_freq = ROPE_THETA ** (-jnp.arange(half, dtype=F32) * 2.0 / ROT_DIM)
    ang = jnp.arange(seq, dtype=jnp.int32).astype(F32)[:, None] * inv_freq[None, :]
    cos, sin = jnp.cos(ang), jnp.sin(ang)
    rest = MOBA_HEAD_DIM - ROT_DIM
    cos_t = jnp.concatenate([cos, cos, jnp.ones((seq, rest), F32)], axis=-1)
    sa_t = jnp.concatenate([-sin, jnp.zeros((seq, MOBA_HEAD_DIM - half), F32)], axis=-1)
    sb_t = jnp.concatenate([jnp.zeros((seq, half), F32), sin, jnp.zeros((seq, rest), F32)], axis=-1)
    return cos_t, sa_t, sb_t


def _cumsum_matrix():
    T, C = HGRN_TILE, HGRN_CHUNK
    r = jnp.arange(T)
    same = (r[:, None] // C) == (r[None, :] // C)
    tril = same & (r[None, :] <= r[:, None])
    return jnp.concatenate([tril, same], axis=0).astype(BF16)


def kernel(x, mem, norm_mix, w_in, lb_logits, hgrn_norm, w_out, norm_xattn, norm_mem,
           w_xq, w_xk, w_xv, w_xo, norm_mlp, w_ff1, w_ff2, norm_final):
    batch, seq, _ = x.shape
    n = batch * seq
    xf = x.reshape(n, D_MODEL)
    lb_table = jnp.cumsum(jax.nn.softmax(lb_logits.astype(F32), axis=0), axis=0)
    cos_t, sa_t, sb_t = _rotary_tables(seq)
    cum = _cumsum_matrix()
    row = lambda v: v.reshape(1, -1)
    depth = norm_mix.shape[0]
    for l in range(depth):
        ph, pm = _inproj(xf, row(norm_mix[l]), w_in[l].astype(BF16), cos_t, sa_t, sb_t, seq)
        o_hgrn = _hgrn(ph, row(lb_table[l]), row(hgrn_norm[l]), cum, batch, seq)
        n_blocks = seq // MOBA_BLOCK
        vt = pm[:, 2 * MOBA_WIDTH:].reshape(batch, n_blocks, MOBA_BLOCK, MOBA_HEADS, MOBA_HEAD_DIM)
        vt = vt.transpose(0, 3, 1, 4, 2)
        o_moba = _moba(pm, vt, batch, seq)
        kmem, vmem = _memkv(mem.reshape(-1, D_MODEL), row(norm_mem[l]),
                            w_xk[l].astype(BF16), w_xv[l].astype(BF16))
        x2 = _mix_xattn(xf, o_hgrn, o_moba, w_out[l].astype(BF16), row(norm_xattn[l]),
                        w_xq[l].astype(BF16), kmem, vmem, w_xo[l].astype(BF16), batch, seq)
        xf = _mlp(x2, row(norm_mlp[l]), w_ff1[l].astype(BF16), w_ff2[l].astype(BF16),
                  row(norm_final), final_norm=(l == depth - 1))
    return xf.reshape(batch, seq, D_MODEL)
```

```python
import functools
import math

import jax
import jax.numpy as jnp
from jax import lax
from jax.experimental import pallas as pl
from jax.experimental.pallas import tpu as pltpu

F32 = jnp.float32
BF16 = jnp.bfloat16

D_MODEL = 1024
MEM_LEN = 256
HGRN_HEADS = 4
HGRN_WIDTH = 512
HGRN_HEAD_DIM = 128
HGRN_CHUNK = 32
MOBA_HEADS = 4
MOBA_WIDTH = 512
MOBA_HEAD_DIM = 128
MOBA_BLOCK = 256
MOBA_TOP_K = 3
ROPE_THETA = 500000.0
ROT_DIM = 32
XATTN_HEADS = 4
XATTN_HEAD_DIM = 256
D_FF = 4096
NORM_EPS = 1e-6
IN_COLS = 4 * HGRN_WIDTH + 3 * MOBA_WIDTH

LOG2E = math.log2(math.e)
NEG_BIG = -1e30

VMEM_LIMIT_BYTES = 56 * 1024 * 1024

BF16_SUBLANES = 16
HGRN_TILE = 256
TOKEN_TILE = 512


def _rms(x, g):
    return x * lax.rsqrt(jnp.mean(x * x, axis=-1, keepdims=True) + NORM_EPS) * g


def _dot(a, b):
    return jnp.dot(a, b, preferred_element_type=F32)


def _dot_nt(a, b):
    return lax.dot_general(a, b, (((1,), (1,)), ((), ())), preferred_element_type=F32)


def _const_spec(shape):
    zeros = (0,) * len(shape)
    return pl.BlockSpec(shape, lambda *_: zeros)


def _inproj_kernel(x_ref, g_ref, w_ref, cos_ref, sa_ref, sb_ref, ph_ref, pm_ref):
    h = _rms(x_ref[...], g_ref[...]).astype(BF16)
    hg_cols = 4 * HGRN_WIDTH
    ph_ref[...] = _dot(h, w_ref[:, :hg_cols])
    cos, sa, sb = cos_ref[...], sa_ref[...], sb_ref[...]
    q_scale = MOBA_HEAD_DIM ** -0.5 * LOG2E
    for part in range(2):
        for hd in range(MOBA_HEADS):
            lo = hg_cols + part * MOBA_WIDTH + hd * MOBA_HEAD_DIM
            t = _dot(h, w_ref[:, lo:lo + MOBA_HEAD_DIM])
            r = (t * cos + pltpu.roll(t, MOBA_HEAD_DIM - ROT_DIM // 2, 1) * sa
                 + pltpu.roll(t, ROT_DIM // 2, 1) * sb)
            if part == 0:
                r = r * q_scale
            o = part * MOBA_WIDTH + hd * MOBA_HEAD_DIM
            pm_ref[:, o:o + MOBA_HEAD_DIM] = r.astype(BF16)
    lo = hg_cols + 2 * MOBA_WIDTH
    pm_ref[:, 2 * MOBA_WIDTH:] = _dot(h, w_ref[:, lo:]).astype(BF16)


def _inproj(xf, g, w, cos_t, sa_t, sb_t, seq):
    n = xf.shape[0]
    tm = TOKEN_TILE
    tiles_per_seq = seq // tm
    return pl.pallas_call(
        _inproj_kernel,
        grid=(n // tm,),
        in_specs=[
            pl.BlockSpec((tm, D_MODEL), lambda i: (i, 0)),
            _const_spec((1, D_MODEL)),
            _const_spec((D_MODEL, IN_COLS)),
            pl.BlockSpec((tm, MOBA_HEAD_DIM), lambda i: (i % tiles_per_seq, 0)),
            pl.BlockSpec((tm, MOBA_HEAD_DIM), lambda i: (i % tiles_per_seq, 0)),
            pl.BlockSpec((tm, MOBA_HEAD_DIM), lambda i: (i % tiles_per_seq, 0)),
        ],
        out_specs=[
            pl.BlockSpec((tm, 4 * HGRN_WIDTH), lambda i: (i, 0)),
            pl.BlockSpec((tm, 3 * MOBA_WIDTH), lambda i: (i, 0)),
        ],
        out_shape=[
            jax.ShapeDtypeStruct((n, 4 * HGRN_WIDTH), F32),
            jax.ShapeDtypeStruct((n, 3 * MOBA_WIDTH), BF16),
        ],
        compiler_params=pltpu.CompilerParams(
            dimension_semantics=("arbitrary",), vmem_limit_bytes=VMEM_LIMIT_BYTES),
        name="inproj",
    )(xf, g, w, cos_t, sa_t, sb_t)


def _hgrn_kernel(ph_ref, lb_ref, ng_ref, cum_ref, o_ref,
                 st_ref, qt_ref, kt_ref, ke_ref, dec_ref, oi_ref):
    T, W, C, dh = HGRN_TILE, HGRN_WIDTH, HGRN_CHUNK, HGRN_HEAD_DIM
    n_chunks = T // C

    @pl.when(pl.program_id(1) == 0)
    def _():
        st_ref[...] = jnp.zeros_like(st_ref)

    hq = ph_ref[:, 0:W]
    hf = ph_ref[:, W:2 * W]
    lb = lb_ref[...]
    f = lb + (1.0 - lb) * jax.nn.sigmoid(hf)
    logf = jnp.log(f)
    kk = 1.0 - f

    l1 = logf.astype(BF16)
    r1 = logf - l1.astype(F32)
    l2 = r1.astype(BF16)
    l3 = (r1 - l2.astype(F32)).astype(BF16)
    cum = cum_ref[...]
    bb = _dot(cum, l1) + _dot(cum, l2) + _dot(cum, l3)
    b = bb[:T]
    b_last = bb[T:]

    qt_ref[...] = (hq * jnp.exp(b)).astype(BF16)
    kt_ref[...] = (kk * jnp.exp(-b)).astype(BF16)
    ke_ref[...] = (kk * jnp.exp(b_last - b)).astype(BF16)
    dec_ref[...] = jnp.exp(b_last)

    row = lax.broadcasted_iota(jnp.int32, (T, T), 0)
    col = lax.broadcasted_iota(jnp.int32, (T, T), 1)
    intra_mask = (row // C == col // C) & (col <= row)
    chunk_of_lane = lax.broadcasted_iota(jnp.int32, (dh, T), 1) // C

    for h in range(HGRN_HEADS):
        hs = slice(h * dh, (h + 1) * dh)
        vf = ph_ref[:, 2 * W + h * dh:2 * W + (h + 1) * dh]
        a = jnp.where(intra_mask, _dot_nt(qt_ref[:, hs], kt_ref[:, hs]), 0.0)
        oi_ref[...] = _dot(a.astype(BF16), vf.astype(BF16))
        v_t = vf.T
        ke = ke_ref[:, hs]
        st = st_ref[h]
        for c in range(n_chunks):
            cs = slice(c * C, (c + 1) * C)
            oi_ref[cs, :] += _dot_nt(qt_ref[cs, hs], st.astype(BF16))
            v_tc = jnp.where(chunk_of_lane == c, v_t, 0.0).astype(BF16)
            upd = _dot(v_tc, ke)
            st = st * dec_ref[c * C:c * C + 1, hs] + upd
        st_ref[h] = st
        o_h = oi_ref[...]
        o_h = o_h * lax.rsqrt(jnp.mean(o_h * o_h, axis=-1, keepdims=True) + NORM_EPS)
        g = ph_ref[:, 3 * W + h * dh:3 * W + (h + 1) * dh]
        o_h = o_h * ng_ref[:, hs] * (g * jax.nn.sigmoid(g))
        o_ref[:, hs] = o_h.astype(o_ref.dtype)


def _hgrn(ph, lb, ng, cum, batch, seq):
    n = ph.shape[0]
    T = HGRN_TILE
    steps = seq // T
    return pl.pallas_call(
        _hgrn_kernel,
        grid=(batch, steps),
        in_specs=[
            pl.BlockSpec((T, 4 * HGRN_WIDTH), lambda b, i: (b * steps + i, 0)),
            _const_spec((1, HGRN_WIDTH)),
            _const_spec((1, HGRN_WIDTH)),
            _const_spec((2 * T, T)),
        ],
        out_specs=pl.BlockSpec((T, HGRN_WIDTH), lambda b, i: (b * steps + i, 0)),
        out_shape=jax.ShapeDtypeStruct((n, HGRN_WIDTH), BF16),
        scratch_shapes=[
            pltpu.VMEM((HGRN_HEADS, HGRN_HEAD_DIM, HGRN_HEAD_DIM), F32),
            pltpu.VMEM((T, HGRN_WIDTH), BF16),
            pltpu.VMEM((T, HGRN_WIDTH), BF16),
            pltpu.VMEM((T, HGRN_WIDTH), BF16),
            pltpu.VMEM((T, HGRN_WIDTH), F32),
            pltpu.VMEM((T, HGRN_HEAD_DIM), F32),
        ],
        compiler_params=pltpu.CompilerParams(
            dimension_semantics=("arbitrary", "arbitrary"), vmem_limit_bytes=VMEM_LIMIT_BYTES),
        name="hgrn",
    )(ph, lb, ng, cum)


def _moba_kernel(q_ref, k_ref, vt_ref, o_ref, kmean_ref, sel_ref, acc_ref, s_ref):
    BLK, hd, H = MOBA_BLOCK, MOBA_HEAD_DIM, MOBA_HEADS
    n_blocks = k_ref.shape[0] // BLK
    i = pl.program_id(1)

    @pl.when(i == 0)
    def _():
        for j in range(n_blocks):
            kb = k_ref[j * BLK:(j + 1) * BLK, :].astype(F32)
            kmean_ref[j:j + 1, :] = jnp.sum(kb, axis=0, keepdims=True) * (1.0 / BLK)

    blk_id = lax.broadcasted_iota(jnp.int32, (n_blocks, BLK), 0)
    past = blk_id < i
    k_pos = lax.broadcasted_iota(jnp.int32, (BLK, BLK), 0)
    q_pos = lax.broadcasted_iota(jnp.int32, (BLK, BLK), 1)
    causal = k_pos <= q_pos
    own = pl.ds(pl.multiple_of(i * BLK, BLK), BLK)

    m_init = []
    for h in range(H):
        hs = slice(h * hd, (h + 1) * hd)
        q = q_ref[:, hs]
        gate = _dot_nt(kmean_ref[:, hs].astype(BF16), q)
        g = jnp.where(past, gate, -jnp.inf)
        chosen = jnp.zeros(gate.shape, F32)
        for _ in range(MOBA_TOP_K):
            m = jnp.max(g, axis=0, keepdims=True)
            idx = jnp.min(jnp.where(g == m, blk_id, n_blocks), axis=0, keepdims=True)
            hit = blk_id == idx
            chosen = jnp.where(hit, 1.0, chosen)
            g = jnp.where(hit, -jnp.inf, g)
        sel_ref[h] = jnp.where(past, chosen, 0.0)
        s = jnp.where(causal, _dot_nt(k_ref[own, hs], q), NEG_BIG)
        m0 = jnp.max(s, axis=0, keepdims=True)
        acc_ref[h] = _dot(vt_ref[h, i], jnp.exp2(s - m0).astype(BF16))
        m_init.append(m0)

    def scores_into(slot, j, h):
        rows = pl.ds(pl.multiple_of(jnp.minimum(j, n_blocks - 1) * BLK, BLK), BLK)
        hs = slice(h * hd, (h + 1) * hd)
        s_ref[slot, h] = _dot_nt(k_ref[rows, hs], q_ref[:, hs])

    def accumulate_from(slot, j, h, m):
        s = s_ref[slot, h]
        picked = sel_ref[h, pl.ds(j, 1), :] > 0.0
        m_new = jnp.maximum(m, jnp.where(picked, jnp.max(s, axis=0, keepdims=True), NEG_BIG))
        pv = _dot(vt_ref[h, j], jnp.exp2(s - m_new).astype(BF16))
        acc_ref[h] = jnp.exp2(m - m_new) * acc_ref[h] + jnp.where(picked, pv, 0.0)
        return m_new

    @pl.when(i > 0)
    def _():
        for h in range(H):
            scores_into(0, 0, h)

    def body(t, ms):
        ms = list(ms)
        j0 = 2 * t
        j1 = j0 + 1
        for h in range(H):
            scores_into(1, j1, h)
            ms[h] = accumulate_from(0, j0, h, ms[h])
        for h in range(H):
            scores_into(0, j1 + 1, h)
            ms[h] = accumulate_from(1, j1, h, ms[h])
        return tuple(ms)

    lax.fori_loop(0, (i + 1) // 2, body, tuple(m_init))
    for h in range(H):
        o = acc_ref[h, :hd, :] / acc_ref[h, hd:hd + 1, :]
        o_ref[:, h * hd:(h + 1) * hd] = o.T.astype(o_ref.dtype)


def _moba(pm, vt, batch, seq):
    n = pm.shape[0]
    BLK, hd, H = MOBA_BLOCK, MOBA_HEAD_DIM, MOBA_HEADS
    n_blocks = seq // BLK
    vt_rows = vt.shape[-2]
    return pl.pallas_call(
        _moba_kernel,
        grid=(batch, n_blocks),
        in_specs=[
            pl.BlockSpec((BLK, H * hd), lambda b, i: (b * n_blocks + i, 0)),
            pl.BlockSpec((seq, H * hd), lambda b, i: (b, 1)),
            pl.BlockSpec((None, H, n_blocks, vt_rows, BLK), lambda b, i: (b, 0, 0, 0, 0)),
        ],
        out_specs=pl.BlockSpec((BLK, H * hd), lambda b, i: (b * n_blocks + i, 0)),
        out_shape=jax.ShapeDtypeStruct((n, MOBA_WIDTH), BF16),
        scratch_shapes=[
            pltpu.VMEM((n_blocks, H * hd), F32),
            pltpu.VMEM((H, n_blocks, BLK), F32),
            pltpu.VMEM((H, vt_rows, BLK), F32),
            pltpu.VMEM((2, H, BLK, BLK), F32),
        ],
        compiler_params=pltpu.CompilerParams(
            dimension_semantics=("arbitrary", "arbitrary"),
            vmem_limit_bytes=VMEM_LIMIT_BYTES),
        name="moba",
    )(pm, pm, vt)


def _memkv_kernel(mem_ref, g_ref, wk_ref, wv_ref, k_ref, v_ref):
    h = _rms(mem_ref[...], g_ref[...]).astype(BF16)
    k_ref[...] = _dot(h, wk_ref[...]).astype(BF16)
    v_ref[...] = _dot(h, wv_ref[...]).astype(BF16)


def _memkv(memf, g, wk, wv):
    n = memf.shape[0]
    tm = MEM_LEN
    return pl.pallas_call(
        _memkv_kernel,
        grid=(n // tm,),
        in_specs=[
            pl.BlockSpec((tm, D_MODEL), lambda i: (i, 0)),
            _const_spec((1, D_MODEL)),
            _const_spec((D_MODEL, D_MODEL)),
            _const_spec((D_MODEL, D_MODEL)),
        ],
        out_specs=[pl.BlockSpec((tm, D_MODEL), lambda i: (i, 0))] * 2,
        out_shape=[jax.ShapeDtypeStruct((n, D_MODEL), BF16)] * 2,
        compiler_params=pltpu.CompilerParams(
            dimension_semantics=("arbitrary",), vmem_limit_bytes=VMEM_LIMIT_BYTES),
        name="memkv",
    )(memf, g, wk, wv)


def _mix_xattn_kernel(x_ref, oh_ref, om_ref, wo_ref, g_ref, wq_ref, k_ref, v_ref, wxo_ref, y_ref):
    x1 = (x_ref[...] + _dot(oh_ref[...], wo_ref[:HGRN_WIDTH, :])
          + _dot(om_ref[...], wo_ref[HGRN_WIDTH:, :]))
    h = _rms(x1, g_ref[...]).astype(BF16)
    q = (_dot(h, wq_ref[...]) * (XATTN_HEAD_DIM ** -0.5 * LOG2E)).astype(BF16)
    outs = []
    for hd in range(XATTN_HEADS):
        hs = slice(hd * XATTN_HEAD_DIM, (hd + 1) * XATTN_HEAD_DIM)
        s = _dot_nt(q[:, hs], k_ref[:, hs])
        p = jnp.exp2(s - jnp.max(s, axis=-1, keepdims=True))
        l = jnp.sum(p, axis=-1, keepdims=True)
        outs.append((_dot(p.astype(BF16), v_ref[:, hs]) / l).astype(BF16))
    o = jnp.concatenate(outs, axis=-1)
    y_ref[...] = x1 + _dot(o, wxo_ref[...])


def _mix_xattn(xf, oh, om, wo, g, wq, kmem, vmem, wxo, batch, seq):
    n = xf.shape[0]
    tm = TOKEN_TILE
    steps = seq // tm
    tok = lambda w: pl.BlockSpec((tm, w), lambda b, i: (b * steps + i, 0))
    return pl.pallas_call(
        _mix_xattn_kernel,
        grid=(batch, steps),
        in_specs=[
            tok(D_MODEL), tok(HGRN_WIDTH), tok(MOBA_WIDTH),
            _const_spec((D_MODEL, D_MODEL)),
            _const_spec((1, D_MODEL)),
            _const_spec((D_MODEL, D_MODEL)),
            pl.BlockSpec((MEM_LEN, D_MODEL), lambda b, i: (b, 0)),
            pl.BlockSpec((MEM_LEN, D_MODEL), lambda b, i: (b, 0)),
            _const_spec((D_MODEL, D_MODEL)),
        ],
        out_specs=tok(D_MODEL),
        out_shape=jax.ShapeDtypeStruct((n, D_MODEL), F32),
        compiler_params=pltpu.CompilerParams(
            dimension_semantics=("arbitrary", "arbitrary"), vmem_limit_bytes=VMEM_LIMIT_BYTES),
        name="mix_xattn",
    )(xf, oh, om, wo, g, wq, kmem, vmem, wxo)


def _mlp_kernel(x_ref, g_ref, w1_ref, w2_ref, gf_ref, y_ref, *, final_norm):
    x = x_ref[...]
    h = _rms(x, g_ref[...]).astype(BF16)
    ff_chunk = D_MODEL
    acc = x
    for c in range(D_FF // ff_chunk):
        cs = slice(c * ff_chunk, (c + 1) * ff_chunk)
        z = jnp.maximum(_dot(h, w1_ref[:, cs]), 0.0)
        acc = acc + _dot((z * z).astype(BF16), w2_ref[cs, :])
    y_ref[...] = _rms(acc, gf_ref[...]) if final_norm else acc


def _mlp(xf, g, w1, w2, gf, final_norm):
    n = xf.shape[0]
    tm = TOKEN_TILE
    return pl.pallas_call(
        functools.partial(_mlp_kernel, final_norm=final_norm),
        grid=(n // tm,),
        in_specs=[
            pl.BlockSpec((tm, D_MODEL), lambda i: (i, 0)),
            _const_spec((1, D_MODEL)),
            _const_spec((D_MODEL, D_FF)),
            _const_spec((D_FF, D_MODEL)),
            _const_spec((1, D_MODEL)),
        ],
        out_specs=pl.BlockSpec((tm, D_MODEL), lambda i: (i, 0)),
        out_shape=jax.ShapeDtypeStruct((n, D_MODEL), F32),
        compiler_params=pltpu.CompilerParams(
            dimension_semantics=("arbitrary",), vmem_limit_bytes=VMEM_LIMIT_BYTES),
        name="mlp",
    )(xf, g, w1, w2, gf)


def _rotary_tables(seq):
    half = ROT_DIM // 2
    inv_freq = ROPE_THETA ** (-jnp.arange(half, dtype=F32) * 2.0 / ROT_DIM)
    ang = jnp.arange(seq, dtype=jnp.int32).astype(F32)[:, None] * inv_freq[None, :]
    cos, sin = jnp.cos(ang), jnp.sin(ang)
    rest = MOBA_HEAD_DIM - ROT_DIM
    cos_t = jnp.concatenate([cos, cos, jnp.ones((seq, rest), F32)], axis=-1)
    sa_t = jnp.concatenate([-sin, jnp.zeros((seq, MOBA_HEAD_DIM - half), F32)], axis=-1)
    sb_t = jnp.concatenate([jnp.zeros((seq, half), F32), sin, jnp.zeros((seq, rest), F32)], axis=-1)
    return cos_t, sa_t, sb_t


def _cumsum_matrix():
    T, C = HGRN_TILE, HGRN_CHUNK
    r = jnp.arange(T)
    same = (r[:, None] // C) == (r[None, :] // C)
    tril = same & (r[None, :] <= r[:, None])
    return jnp.concatenate([tril, same], axis=0).astype(BF16)


def kernel(x, mem, norm_mix, w_in, lb_logits, hgrn_norm, w_out, norm_xattn, norm_mem,
           w_xq, w_xk, w_xv, w_xo, norm_mlp, w_ff1, w_ff2, norm_final):
    batch, seq, _ = x.shape
    n = batch * seq
    xf = x.reshape(n, D_MODEL)
    lb_table = jnp.cumsum(jax.nn.softmax(lb_logits.astype(F32), axis=0), axis=0)
    cos_t, sa_t, sb_t = _rotary_tables(seq)
    cum = _cumsum_matrix()
    row = lambda v: v.reshape(1, -1)
    depth = norm_mix.shape[0]
    for l in range(depth):
        ph, pm = _inproj(xf, row(norm_mix[l]), w_in[l].astype(BF16), cos_t, sa_t, sb_t, seq)
        o_hgrn = _hgrn(ph, row(lb_table[l]), row(hgrn_norm[l]), cum, batch, seq)
        n_blocks = seq // MOBA_BLOCK
        vt = pm[:, 2 * MOBA_WIDTH:].reshape(batch, n_blocks, MOBA_BLOCK, MOBA_HEADS, MOBA_HEAD_DIM)
        vt = vt.transpose(0, 3, 1, 4, 2)
        ones_rows = jnp.ones(vt.shape[:3] + (BF16_SUBLANES, MOBA_BLOCK), BF16)
        vt = jnp.concatenate([vt, ones_rows], axis=3)
        o_moba = _moba(pm, vt, batch, seq)
        kmem, vmem = _memkv(mem.reshape(-1, D_MODEL), row(norm_mem[l]),
                            w_xk[l].astype(BF16), w_xv[l].astype(BF16))
        x2 = _mix_xattn(xf, o_hgrn, o_moba, w_out[l].astype(BF16), row(norm_xattn[l]),
                        w_xq[l].astype(BF16), kmem, vmem, w_xo[l].astype(BF16), batch, seq)
        xf = _mlp(x2, row(norm_mlp[l]), w_ff1[l].astype(BF16), w_ff2[l].astype(BF16),
                  row(norm_final), final_norm=(l == depth - 1))
    return xf.reshape(batch, seq, D_MODEL)
```

```python
import functools
import math

import jax
import jax.numpy as jnp
import numpy as np
from jax import lax
from jax.experimental import pallas as pl
from jax.experimental.pallas import tpu as pltpu

F32 = jnp.float32
BF16 = jnp.bfloat16

D_MODEL = 1024
MEM_LEN = 256
HGRN_HEADS = 4
HGRN_WIDTH = 512
HGRN_HEAD_DIM = 128
HGRN_CHUNK = 32
MOBA_HEADS = 4
MOBA_WIDTH = 512
MOBA_HEAD_DIM = 128
MOBA_BLOCK = 256
MOBA_TOP_K = 3
ROPE_THETA = 500000.0
ROT_DIM = 32
XATTN_HEADS = 4
XATTN_HEAD_DIM = 256
D_FF = 4096
NORM_EPS = 1e-6
IN_COLS = 4 * HGRN_WIDTH + 3 * MOBA_WIDTH

LOG2E = math.log2(math.e)
NEG_BIG = -1e30

VMEM_LIMIT_BYTES = 56 * 1024 * 1024

BF16_SUBLANES = 16
HGRN_TILE = 256
TOKEN_TILE = 512


def _rms(x, g):
    return x * lax.rsqrt(jnp.mean(x * x, axis=-1, keepdims=True) + NORM_EPS) * g


def _dot(a, b):
    return jnp.dot(a, b, preferred_element_type=F32)


def _dot_nt(a, b):
    return lax.dot_general(a, b, (((1,), (1,)), ((), ())), preferred_element_type=F32)


def _const_spec(shape):
    zeros = (0,) * len(shape)
    return pl.BlockSpec(shape, lambda *_: zeros)


def _inproj_kernel(x_ref, g_ref, w_ref, cos_ref, sa_ref, sb_ref, ph_ref, pm_ref, vt_ref):
    h = _rms(x_ref[...], g_ref[...]).astype(BF16)
    hg_cols = 4 * HGRN_WIDTH
    hd, BLK = MOBA_HEAD_DIM, MOBA_BLOCK
    ph_ref[...] = _dot(h, w_ref[:, :hg_cols])
    qk = _dot(h, w_ref[:, hg_cols:hg_cols + 2 * MOBA_WIDTH])
    cos, sa, sb = cos_ref[...], sa_ref[...], sb_ref[...]
    q_scale = hd ** -0.5 * LOG2E
    for o in range(0, 2 * MOBA_WIDTH, hd):
        t = qk[:, o:o + hd]
        r = t * cos + pltpu.roll(t, hd - ROT_DIM // 2, 1) * sa + pltpu.roll(t, ROT_DIM // 2, 1) * sb
        if o < MOBA_WIDTH:
            r = r * q_scale
        pm_ref[:, o:o + hd] = r.astype(BF16)
    v = _dot(h, w_ref[:, hg_cols + 2 * MOBA_WIDTH:])
    ones_rows = jnp.ones((vt_ref.shape[2] - hd, BLK), BF16)
    for head in range(MOBA_HEADS):
        for blk in range(v.shape[0] // BLK):
            vt_ref[head, blk, :hd, :] = v[blk * BLK:(blk + 1) * BLK, head * hd:(head + 1) * hd].T.astype(BF16)
            vt_ref[head, blk, hd:, :] = ones_rows


def _inproj(xf, g, w, cos_t, sa_t, sb_t, batch, seq):
    n = xf.shape[0]
    tm = TOKEN_TILE
    tiles_per_seq = seq // tm
    blocks_per_tile = tm // MOBA_BLOCK
    vt_rows = MOBA_HEAD_DIM + BF16_SUBLANES
    rot = lambda: pl.BlockSpec((tm, MOBA_HEAD_DIM), lambda i: (i % tiles_per_seq, 0))
    return pl.pallas_call(
        _inproj_kernel,
        grid=(n // tm,),
        in_specs=[
            pl.BlockSpec((tm, D_MODEL), lambda i: (i, 0)),
            _const_spec((1, D_MODEL)),
            _const_spec((D_MODEL, IN_COLS)),
            rot(), rot(), rot(),
        ],
        out_specs=[
            pl.BlockSpec((tm, 4 * HGRN_WIDTH), lambda i: (i, 0)),
            pl.BlockSpec((tm, 2 * MOBA_WIDTH), lambda i: (i, 0)),
            pl.BlockSpec((None, MOBA_HEADS, blocks_per_tile, vt_rows, MOBA_BLOCK),
                         lambda i: (i // tiles_per_seq, 0, i % tiles_per_seq, 0, 0)),
        ],
        out_shape=[
            jax.ShapeDtypeStruct((n, 4 * HGRN_WIDTH), F32),
            jax.ShapeDtypeStruct((n, 2 * MOBA_WIDTH), BF16),
            jax.ShapeDtypeStruct((batch, MOBA_HEADS, seq // MOBA_BLOCK, vt_rows, MOBA_BLOCK), BF16),
        ],
        compiler_params=pltpu.CompilerParams(
            dimension_semantics=("arbitrary",), vmem_limit_bytes=VMEM_LIMIT_BYTES),
        name="inproj",
    )(xf, g, w, cos_t, sa_t, sb_t)


def _hgrn_kernel(ph_ref, lb_ref, ng_ref, cum_ref, o_ref,
                 st_ref, b_ref, kk_ref, qt_ref, kt_ref, keb_ref, dec_ref, oi_ref):
    T, W, C, dh = HGRN_TILE, HGRN_WIDTH, HGRN_CHUNK, HGRN_HEAD_DIM
    n_chunks = T // C

    @pl.when((pl.program_id(0) == 0) & (pl.program_id(1) == 0))
    def _():
        keb_ref[...] = jnp.zeros_like(keb_ref)

    @pl.when(pl.program_id(1) == 0)
    def _():
        st_ref[...] = jnp.zeros_like(st_ref)

    lb = lb_ref[...]
    f = lb + (1.0 - lb) * jax.nn.sigmoid(ph_ref[:, W:2 * W])
    logf = jnp.log(f)
    kk_ref[...] = 1.0 - f

    l1 = logf.astype(BF16)
    l2 = (logf - l1.astype(F32)).astype(BF16)
    cum = cum_ref[...]
    b_ref[...] = _dot(cum, l1) + _dot(cum, l2)

    row = lax.broadcasted_iota(jnp.int32, (T, T), 0)
    col = lax.broadcasted_iota(jnp.int32, (T, T), 1)
    intra_mask = (row // C == col // C) & (col <= row)

    for h in range(HGRN_HEADS):
        hs = slice(h * dh, (h + 1) * dh)
        b = b_ref[:, hs]
        qt_ref[:, hs] = (ph_ref[:, h * dh:(h + 1) * dh] * jnp.exp(b)).astype(BF16)
        kt_ref[:, hs] = (kk_ref[:, hs] * jnp.exp(-b)).astype(BF16)
        for c in range(n_chunks):
            cs = slice(c * C, (c + 1) * C)
            b_last = b_ref[(c + 1) * C - 1:(c + 1) * C, hs]
            dec_ref[c:c + 1, hs] = jnp.exp(b_last)
            keb_ref[h, cs, c * dh:(c + 1) * dh] = (
                kk_ref[cs, hs] * jnp.exp(b_last - b_ref[cs, hs])).astype(BF16)
        vf = ph_ref[:, 2 * W + h * dh:2 * W + (h + 1) * dh]
        a = jnp.where(intra_mask, _dot_nt(qt_ref[:, hs], kt_ref[:, hs]), 0.0)
        upd = _dot(vf.T.astype(BF16), keb_ref[h])
        oi_ref[h] = _dot(a.astype(BF16), vf.astype(BF16))
        st = st_ref[h]
        for c in range(n_chunks):
            cs = slice(c * C, (c + 1) * C)
            oi_ref[h, cs, :] += _dot_nt(qt_ref[cs, hs], st.astype(BF16))
            st = st * dec_ref[c:c + 1, hs] + upd[:, c * dh:(c + 1) * dh]
        st_ref[h] = st
        o_h = oi_ref[h]
        o_h = o_h * lax.rsqrt(jnp.mean(o_h * o_h, axis=-1, keepdims=True) + NORM_EPS)
        g = ph_ref[:, 3 * W + h * dh:3 * W + (h + 1) * dh]
        o_h = o_h * ng_ref[:, hs] * (g * jax.nn.sigmoid(g))
        o_ref[:, hs] = o_h.astype(o_ref.dtype)


def _hgrn(ph, lb, ng, cum, batch, seq):
    n = ph.shape[0]
    T = HGRN_TILE
    steps = seq // T
    n_chunks = T // HGRN_CHUNK
    return pl.pallas_call(
        _hgrn_kernel,
        grid=(batch, steps),
        in_specs=[
            pl.BlockSpec((T, 4 * HGRN_WIDTH), lambda b, i: (b * steps + i, 0)),
            _const_spec((1, HGRN_WIDTH)),
            _const_spec((1, HGRN_WIDTH)),
            _const_spec((T, T)),
        ],
        out_specs=pl.BlockSpec((T, HGRN_WIDTH), lambda b, i: (b * steps + i, 0)),
        out_shape=jax.ShapeDtypeStruct((n, HGRN_WIDTH), BF16),
        scratch_shapes=[
            pltpu.VMEM((HGRN_HEADS, HGRN_HEAD_DIM, HGRN_HEAD_DIM), F32),
            pltpu.VMEM((T, HGRN_WIDTH), F32),
            pltpu.VMEM((T, HGRN_WIDTH), F32),
            pltpu.VMEM((T, HGRN_WIDTH), BF16),
            pltpu.VMEM((T, HGRN_WIDTH), BF16),
            pltpu.VMEM((HGRN_HEADS, T, n_chunks * HGRN_HEAD_DIM), BF16),
            pltpu.VMEM((n_chunks, HGRN_WIDTH), F32),
            pltpu.VMEM((HGRN_HEADS, T, HGRN_HEAD_DIM), F32),
        ],
        compiler_params=pltpu.CompilerParams(
            dimension_semantics=("arbitrary", "arbitrary"), vmem_limit_bytes=VMEM_LIMIT_BYTES),
        name="hgrn",
    )(ph, lb, ng, cum)


def _moba_kernel(q_ref, k_ref, vt_ref, o_ref, kmean_ref, sel_ref, acc_ref, s_ref):
    BLK, hd, H = MOBA_BLOCK, MOBA_HEAD_DIM, MOBA_HEADS
    n_blocks = k_ref.shape[0] // BLK
    i = pl.program_id(1)

    @pl.when(i == 0)
    def _():
        for j in range(n_blocks):
            kb = k_ref[j * BLK:(j + 1) * BLK, :].astype(F32)
            kmean_ref[j:j + 1, :] = jnp.sum(kb, axis=0, keepdims=True) * (1.0 / BLK)

    blk_id = lax.broadcasted_iota(jnp.int32, (n_blocks, BLK), 0)
    past = blk_id < i
    k_pos = lax.broadcasted_iota(jnp.int32, (BLK, BLK), 0)
    q_pos = lax.broadcasted_iota(jnp.int32, (BLK, BLK), 1)
    causal = k_pos <= q_pos
    own = pl.ds(pl.multiple_of(i * BLK, BLK), BLK)

    m_init = []
    for h in range(H):
        hs = slice(h * hd, (h + 1) * hd)
        q = q_ref[:, hs]
        gate = _dot_nt(kmean_ref[:, hs].astype(BF16), q)
        g = jnp.where(past, gate, -jnp.inf)
        chosen = jnp.zeros(gate.shape, F32)
        for _ in range(MOBA_TOP_K):
            m = jnp.max(g, axis=0, keepdims=True)
            idx = jnp.min(jnp.where(g == m, blk_id, n_blocks), axis=0, keepdims=True)
            hit = blk_id == idx
            chosen = jnp.where(hit, 1.0, chosen)
            g = jnp.where(hit, -jnp.inf, g)
        sel_ref[h] = jnp.where(past, chosen, 0.0)
        s = jnp.where(causal, _dot_nt(k_ref[own, hs], q), NEG_BIG)
        m0 = jnp.max(s, axis=0, keepdims=True)
        acc_ref[h] = _dot(vt_ref[h, i], jnp.exp2(s - m0).astype(BF16))
        m_init.append(m0)

    def scores_into(slot, j, h):
        rows = pl.ds(pl.multiple_of(jnp.minimum(j, n_blocks - 1) * BLK, BLK), BLK)
        hs = slice(h * hd, (h + 1) * hd)
        s_ref[slot, h] = _dot_nt(k_ref[rows, hs], q_ref[:, hs])

    def accumulate_from(slot, j, h, m):
        s = s_ref[slot, h]
        picked = sel_ref[h, pl.ds(j, 1), :] > 0.0
        m_new = jnp.maximum(m, jnp.where(picked, jnp.max(s, axis=0, keepdims=True), NEG_BIG))
        pv = _dot(vt_ref[h, j], jnp.exp2(s - m_new).astype(BF16))
        acc_ref[h] = jnp.exp2(m - m_new) * acc_ref[h] + jnp.where(picked, pv, 0.0)
        return m_new

    @pl.when(i > 0)
    def _():
        for h in range(H):
            scores_into(0, 0, h)

    def body(t, ms):
        ms = list(ms)
        j0 = 2 * t
        j1 = j0 + 1
        for h in range(H):
            scores_into(1, j1, h)
            ms[h] = accumulate_from(0, j0, h, ms[h])
        for h in range(H):
            scores_into(0, j1 + 1, h)
            ms[h] = accumulate_from(1, j1, h, ms[h])
        return tuple(ms)

    lax.fori_loop(0, (i + 1) // 2, body, tuple(m_init))
    for h in range(H):
        o = acc_ref[h, :hd, :] / acc_ref[h, hd:hd + 1, :]
        o_ref[:, h * hd:(h + 1) * hd] = o.T.astype(o_ref.dtype)


def _moba(pm, vt, batch, seq):
    n = pm.shape[0]
    BLK, hd, H = MOBA_BLOCK, MOBA_HEAD_DIM, MOBA_HEADS
    n_blocks = seq // BLK
    vt_rows = vt.shape[-2]
    return pl.pallas_call(
        _moba_kernel,
        grid=(batch, n_blocks),
        in_specs=[
            pl.BlockSpec((BLK, H * hd), lambda b, i: (b * n_blocks + i, 0)),
            pl.BlockSpec((seq, H * hd), lambda b, i: (b, 1)),
            pl.BlockSpec((None, H, n_blocks, vt_rows, BLK), lambda b, i: (b, 0, 0, 0, 0)),
        ],
        out_specs=pl.BlockSpec((BLK, H * hd), lambda b, i: (b * n_blocks + i, 0)),
        out_shape=jax.ShapeDtypeStruct((n, MOBA_WIDTH), BF16),
        scratch_shapes=[
            pltpu.VMEM((n_blocks, H * hd), F32),
            pltpu.VMEM((H, n_blocks, BLK), F32),
            pltpu.VMEM((H, vt_rows, BLK), F32),
            pltpu.VMEM((2, H, BLK, BLK), F32),
        ],
        compiler_params=pltpu.CompilerParams(
            dimension_semantics=("arbitrary", "arbitrary"),
            vmem_limit_bytes=VMEM_LIMIT_BYTES),
        name="moba",
    )(pm, pm, vt)


def _memkv_kernel(mem_ref, g_ref, wk_ref, wv_ref, k_ref, v_ref):
    h = _rms(mem_ref[...], g_ref[...]).astype(BF16)
    k_ref[...] = _dot(h, wk_ref[...]).astype(BF16)
    v_ref[...] = _dot(h, wv_ref[...]).astype(BF16)


def _memkv(memf, g, wk, wv):
    n = memf.shape[0]
    tm = MEM_LEN
    return pl.pallas_call(
        _memkv_kernel,
        grid=(n // tm,),
        in_specs=[
            pl.BlockSpec((tm, D_MODEL), lambda i: (i, 0)),
            _const_spec((1, D_MODEL)),
            _const_spec((D_MODEL, D_MODEL)),
            _const_spec((D_MODEL, D_MODEL)),
        ],
        out_specs=[pl.BlockSpec((tm, D_MODEL), lambda i: (i, 0))] * 2,
        out_shape=[jax.ShapeDtypeStruct((n, D_MODEL), BF16)] * 2,
        compiler_params=pltpu.CompilerParams(
            dimension_semantics=("arbitrary",), vmem_limit_bytes=VMEM_LIMIT_BYTES),
        name="memkv",
    )(memf, g, wk, wv)


def _mix_xattn_kernel(x_ref, oh_ref, om_ref, wo_ref, g_ref, wq_ref, k_ref, v_ref, wxo_ref, y_ref):
    x1 = (x_ref[...] + _dot(oh_ref[...], wo_ref[:HGRN_WIDTH, :])
          + _dot(om_ref[...], wo_ref[HGRN_WIDTH:, :]))
    h = _rms(x1, g_ref[...]).astype(BF16)
    q = (_dot(h, wq_ref[...]) * (XATTN_HEAD_DIM ** -0.5 * LOG2E)).astype(BF16)
    outs = []
    for hd in range(XATTN_HEADS):
        hs = slice(hd * XATTN_HEAD_DIM, (hd + 1) * XATTN_HEAD_DIM)
        s = _dot_nt(q[:, hs], k_ref[:, hs])
        p = jnp.exp2(s - jnp.max(s, axis=-1, keepdims=True))
        l = jnp.sum(p, axis=-1, keepdims=True)
        outs.append((_dot(p.astype(BF16), v_ref[:, hs]) / l).astype(BF16))
    o = jnp.concatenate(outs, axis=-1)
    y_ref[...] = x1 + _dot(o, wxo_ref[...])


def _mix_xattn(xf, oh, om, wo, g, wq, kmem, vmem, wxo, batch, seq):
    n = xf.shape[0]
    tm = TOKEN_TILE
    steps = seq // tm
    tok = lambda w: pl.BlockSpec((tm, w), lambda b, i: (b * steps + i, 0))
    return pl.pallas_call(
        _mix_xattn_kernel,
        grid=(batch, steps),
        in_specs=[
            tok(D_MODEL), tok(HGRN_WIDTH), tok(MOBA_WIDTH),
            _const_spec((D_MODEL, D_MODEL)),
            _const_spec((1, D_MODEL)),
            _const_spec((D_MODEL, D_MODEL)),
            pl.BlockSpec((MEM_LEN, D_MODEL), lambda b, i: (b, 0)),
            pl.BlockSpec((MEM_LEN, D_MODEL), lambda b, i: (b, 0)),
            _const_spec((D_MODEL, D_MODEL)),
        ],
        out_specs=tok(D_MODEL),
        out_shape=jax.ShapeDtypeStruct((n, D_MODEL), F32),
        compiler_params=pltpu.CompilerParams(
            dimension_semantics=("arbitrary", "arbitrary"), vmem_limit_bytes=VMEM_LIMIT_BYTES),
        name="mix_xattn",
    )(xf, oh, om, wo, g, wq, kmem, vmem, wxo)


def _mlp_kernel(x_ref, g_ref, w1_ref, w2_ref, gf_ref, y_ref, *, final_norm):
    x = x_ref[...]
    h = _rms(x, g_ref[...]).astype(BF16)
    ff_chunk = D_MODEL
    acc = x
    for c in range(D_FF // ff_chunk):
        cs = slice(c * ff_chunk, (c + 1) * ff_chunk)
        z = jnp.maximum(_dot(h, w1_ref[:, cs]), 0.0)
        acc = acc + _dot((z * z).astype(BF16), w2_ref[cs, :])
    y_ref[...] = _rms(acc, gf_ref[...]) if final_norm else acc


def _mlp(xf, g, w1, w2, gf, final_norm):
    n = xf.shape[0]
    tm = TOKEN_TILE
    return pl.pallas_call(
        functools.partial(_mlp_kernel, final_norm=final_norm),
        grid=(n // tm,),
        in_specs=[
            pl.BlockSpec((tm, D_MODEL), lambda i: (i, 0)),
            _const_spec((1, D_MODEL)),
            _const_spec((D_MODEL, D_FF)),
            _const_spec((D_FF, D_MODEL)),
            _const_spec((1, D_MODEL)),
        ],
        out_specs=pl.BlockSpec((tm, D_MODEL), lambda i: (i, 0)),
        out_shape=jax.ShapeDtypeStruct((n, D_MODEL), F32),
        compiler_params=pltpu.CompilerParams(
            dimension_semantics=("arbitrary",), vmem_limit_bytes=VMEM_LIMIT_BYTES),
        name="mlp",
    )(xf, g, w1, w2, gf)


def _rotary_tables(seq):
    half = ROT_DIM // 2
    inv_freq = np.float32(ROPE_THETA) ** (-np.arange(half, dtype=np.float32) * np.float32(2.0 / ROT_DIM))
    ang = (np.arange(seq, dtype=np.float32)[:, None] * inv_freq[None, :]).astype(np.float64)
    cos, sin = np.cos(ang), np.sin(ang)
    rest = MOBA_HEAD_DIM - ROT_DIM
    cos_t = np.concatenate([cos, cos, np.ones((seq, rest))], axis=-1)
    sa_t = np.concatenate([-sin, np.zeros((seq, MOBA_HEAD_DIM - half))], axis=-1)
    sb_t = np.concatenate([np.zeros((seq, half)), sin, np.zeros((seq, rest))], axis=-1)
    return tuple(jnp.asarray(t, F32) for t in (cos_t, sa_t, sb_t))


def _cumsum_matrix():
    T, C = HGRN_TILE, HGRN_CHUNK
    r = np.arange(T)
    same = (r[:, None] // C) == (r[None, :] // C)
    return jnp.asarray(same & (r[None, :] <= r[:, None]), BF16)


def kernel(x, mem, norm_mix, w_in, lb_logits, hgrn_norm, w_out, norm_xattn, norm_mem,
           w_xq, w_xk, w_xv, w_xo, norm_mlp, w_ff1, w_ff2, norm_final):
    batch, seq, _ = x.shape
    n = batch * seq
    xf = x.reshape(n, D_MODEL)
    lb_table = jnp.cumsum(jax.nn.softmax(lb_logits.astype(F32), axis=0), axis=0)
    cos_t, sa_t, sb_t = _rotary_tables(seq)
    cum = _cumsum_matrix()
    row = lambda v: v.reshape(1, -1)
    depth = norm_mix.shape[0]
    for l in range(depth):
        ph, pm, vt = _inproj(xf, row(norm_mix[l]), w_in[l].astype(BF16), cos_t, sa_t, sb_t,
                             batch, seq)
        o_hgrn = _hgrn(ph, row(lb_table[l]), row(hgrn_norm[l]), cum, batch, seq)
        o_moba = _moba(pm, vt, batch, seq)
        kmem, vmem = _memkv(mem.reshape(-1, D_MODEL), row(norm_mem[l]),
                            w_xk[l].astype(BF16), w_xv[l].astype(BF16))
        x2 = _mix_xattn(xf, o_hgrn, o_moba, w_out[l].astype(BF16), row(norm_xattn[l]),
                        w_xq[l].astype(BF16), kmem, vmem, w_xo[l].astype(BF16), batch, seq)
        xf = _mlp(x2, row(norm_mlp[l]), w_ff1[l].astype(BF16), w_ff2[l].astype(BF16),
                  row(norm_final), final_norm=(l == depth - 1))
    return xf.reshape(batch, seq, D_MODEL)
```

```python
import functools
import math

import jax
import jax.numpy as jnp
import numpy as np
from jax import lax
from jax.experimental import pallas as pl
from jax.experimental.pallas import tpu as pltpu

F32 = jnp.float32
BF16 = jnp.bfloat16

D_MODEL = 1024
MEM_LEN = 256
HGRN_HEADS = 4
HGRN_WIDTH = 512
HGRN_HEAD_DIM = 128
HGRN_CHUNK = 32
MOBA_HEADS = 4
MOBA_WIDTH = 512
MOBA_HEAD_DIM = 128
MOBA_BLOCK = 256
MOBA_TOP_K = 3
ROPE_THETA = 500000.0
ROT_DIM = 32
XATTN_HEADS = 4
XATTN_HEAD_DIM = 256
D_FF = 4096
NORM_EPS = 1e-6
IN_COLS = 4 * HGRN_WIDTH + 3 * MOBA_WIDTH

LOG2E = math.log2(math.e)
NEG_BIG = -1e30

VMEM_LIMIT_BYTES = 56 * 1024 * 1024

BF16_SUBLANES = 16
HGRN_TILE = 256
TOKEN_TILE = 512


def _rms(x, g):
    return x * lax.rsqrt(jnp.mean(x * x, axis=-1, keepdims=True) + NORM_EPS) * g


def _dot(a, b):
    return jnp.dot(a, b, preferred_element_type=F32)


def _dot_nt(a, b):
    return lax.dot_general(a, b, (((1,), (1,)), ((), ())), preferred_element_type=F32)


def _const_spec(shape):
    zeros = (0,) * len(shape)
    return pl.BlockSpec(shape, lambda *_: zeros)


def _inproj_kernel(x_ref, g_ref, w_ref, cos_ref, sa_ref, sb_ref, ph_ref, pm_ref, vt_ref):
    h = _rms(x_ref[...], g_ref[...]).astype(BF16)
    hg_cols = 4 * HGRN_WIDTH
    hd, BLK = MOBA_HEAD_DIM, MOBA_BLOCK
    ph_ref[...] = _dot(h, w_ref[:, :hg_cols])
    qk = _dot(h, w_ref[:, hg_cols:hg_cols + 2 * MOBA_WIDTH])
    cos, sa, sb = cos_ref[...], sa_ref[...], sb_ref[...]
    q_scale = hd ** -0.5 * LOG2E
    for o in range(0, 2 * MOBA_WIDTH, hd):
        t = qk[:, o:o + hd]
        r = t * cos + pltpu.roll(t, hd - ROT_DIM // 2, 1) * sa + pltpu.roll(t, ROT_DIM // 2, 1) * sb
        if o < MOBA_WIDTH:
            r = r * q_scale
        pm_ref[:, o:o + hd] = r.astype(BF16)
    v = _dot(h, w_ref[:, hg_cols + 2 * MOBA_WIDTH:])
    ones_rows = jnp.ones((vt_ref.shape[2] - hd, BLK), BF16)
    for head in range(MOBA_HEADS):
        for blk in range(v.shape[0] // BLK):
            vt_ref[head, blk, :hd, :] = v[blk * BLK:(blk + 1) * BLK, head * hd:(head + 1) * hd].T.astype(BF16)
            vt_ref[head, blk, hd:, :] = ones_rows


def _inproj(xf, g, w, cos_t, sa_t, sb_t, batch, seq):
    n = xf.shape[0]
    tm = TOKEN_TILE
    tiles_per_seq = seq // tm
    blocks_per_tile = tm // MOBA_BLOCK
    vt_rows = MOBA_HEAD_DIM + BF16_SUBLANES
    rot = lambda: pl.BlockSpec((tm, MOBA_HEAD_DIM), lambda i: (i % tiles_per_seq, 0))
    return pl.pallas_call(
        _inproj_kernel,
        grid=(n // tm,),
        in_specs=[
            pl.BlockSpec((tm, D_MODEL), lambda i: (i, 0)),
            _const_spec((1, D_MODEL)),
            _const_spec((D_MODEL, IN_COLS)),
            rot(), rot(), rot(),
        ],
        out_specs=[
            pl.BlockSpec((tm, 4 * HGRN_WIDTH), lambda i: (i, 0)),
            pl.BlockSpec((tm, 2 * MOBA_WIDTH), lambda i: (i, 0)),
            pl.BlockSpec((None, MOBA_HEADS, blocks_per_tile, vt_rows, MOBA_BLOCK),
                         lambda i: (i // tiles_per_seq, 0, i % tiles_per_seq, 0, 0)),
        ],
        out_shape=[
            jax.ShapeDtypeStruct((n, 4 * HGRN_WIDTH), F32),
            jax.ShapeDtypeStruct((n, 2 * MOBA_WIDTH), BF16),
            jax.ShapeDtypeStruct((batch, MOBA_HEADS, seq // MOBA_BLOCK, vt_rows, MOBA_BLOCK), BF16),
        ],
        compiler_params=pltpu.CompilerParams(
            dimension_semantics=("arbitrary",), vmem_limit_bytes=VMEM_LIMIT_BYTES),
        name="inproj",
    )(xf, g, w, cos_t, sa_t, sb_t)


def _hgrn_kernel(ph_ref, lb_ref, ng_ref, cum_ref, o_ref,
                 st_ref, b_ref, kk_ref, qt_ref, kt_ref, keb_ref, dec_ref, oi_ref):
    T, W, C, dh = HGRN_TILE, HGRN_WIDTH, HGRN_CHUNK, HGRN_HEAD_DIM
    n_chunks = T // C

    @pl.when((pl.program_id(0) == 0) & (pl.program_id(1) == 0))
    def _():
        keb_ref[...] = jnp.zeros_like(keb_ref)

    @pl.when(pl.program_id(1) == 0)
    def _():
        st_ref[...] = jnp.zeros_like(st_ref)

    lb = lb_ref[...]
    f = lb + (1.0 - lb) * jax.nn.sigmoid(ph_ref[:, W:2 * W])
    logf = jnp.log(f)
    kk_ref[...] = 1.0 - f

    l1 = logf.astype(BF16)
    l2 = (logf - l1.astype(F32)).astype(BF16)
    cum = cum_ref[...]
    b_ref[...] = _dot(cum, l1) + _dot(cum, l2)

    row = lax.broadcasted_iota(jnp.int32, (T, T), 0)
    col = lax.broadcasted_iota(jnp.int32, (T, T), 1)
    intra_mask = (row // C == col // C) & (col <= row)

    for h in range(HGRN_HEADS):
        hs = slice(h * dh, (h + 1) * dh)
        b = b_ref[:, hs]
        qt_ref[:, hs] = (ph_ref[:, h * dh:(h + 1) * dh] * jnp.exp(b)).astype(BF16)
        kt_ref[:, hs] = (kk_ref[:, hs] * jnp.exp(-b)).astype(BF16)
        for c in range(n_chunks):
            cs = slice(c * C, (c + 1) * C)
            b_last = b_ref[(c + 1) * C - 1:(c + 1) * C, hs]
            dec_ref[c:c + 1, hs] = jnp.exp(b_last)
            keb_ref[h, cs, c * dh:(c + 1) * dh] = (
                kk_ref[cs, hs] * jnp.exp(b_last - b_ref[cs, hs])).astype(BF16)
        vf = ph_ref[:, 2 * W + h * dh:2 * W + (h + 1) * dh]
        a = jnp.where(intra_mask, _dot_nt(qt_ref[:, hs], kt_ref[:, hs]), 0.0)
        upd = _dot(vf.T.astype(BF16), keb_ref[h])
        oi_ref[h] = _dot(a.astype(BF16), vf.astype(BF16))
        st = st_ref[h]
        for c in range(n_chunks):
            cs = slice(c * C, (c + 1) * C)
            oi_ref[h, cs, :] += _dot_nt(qt_ref[cs, hs], st.astype(BF16))
            st = st * dec_ref[c:c + 1, hs] + upd[:, c * dh:(c + 1) * dh]
        st_ref[h] = st
        o_h = oi_ref[h]
        o_h = o_h * lax.rsqrt(jnp.mean(o_h * o_h, axis=-1, keepdims=True) + NORM_EPS)
        g = ph_ref[:, 3 * W + h * dh:3 * W + (h + 1) * dh]
        o_h = o_h * ng_ref[:, hs] * (g * jax.nn.sigmoid(g))
        o_ref[:, hs] = o_h.astype(o_ref.dtype)


def _hgrn(ph, lb, ng, cum, batch, seq):
    n = ph.shape[0]
    T = HGRN_TILE
    steps = seq // T
    n_chunks = T // HGRN_CHUNK
    return pl.pallas_call(
        _hgrn_kernel,
        grid=(batch, steps),
        in_specs=[
            pl.BlockSpec((T, 4 * HGRN_WIDTH), lambda b, i: (b * steps + i, 0)),
            _const_spec((1, HGRN_WIDTH)),
            _const_spec((1, HGRN_WIDTH)),
            _const_spec((T, T)),
        ],
        out_specs=pl.BlockSpec((T, HGRN_WIDTH), lambda b, i: (b * steps + i, 0)),
        out_shape=jax.ShapeDtypeStruct((n, HGRN_WIDTH), BF16),
        scratch_shapes=[
            pltpu.VMEM((HGRN_HEADS, HGRN_HEAD_DIM, HGRN_HEAD_DIM), F32),
            pltpu.VMEM((T, HGRN_WIDTH), F32),
            pltpu.VMEM((T, HGRN_WIDTH), F32),
            pltpu.VMEM((T, HGRN_WIDTH), BF16),
            pltpu.VMEM((T, HGRN_WIDTH), BF16),
            pltpu.VMEM((HGRN_HEADS, T, n_chunks * HGRN_HEAD_DIM), BF16),
            pltpu.VMEM((n_chunks, HGRN_WIDTH), F32),
            pltpu.VMEM((HGRN_HEADS, T, HGRN_HEAD_DIM), F32),
        ],
        compiler_params=pltpu.CompilerParams(
            dimension_semantics=("arbitrary", "arbitrary"), vmem_limit_bytes=VMEM_LIMIT_BYTES),
        name="hgrn",
    )(ph, lb, ng, cum)


def _moba_kernel(q_ref, k_ref, vt_ref, o_ref, kmean_ref, sel_ref, acc_ref, s_ref):
    BLK, hd, H = MOBA_BLOCK, MOBA_HEAD_DIM, MOBA_HEADS
    n_blocks = k_ref.shape[0] // BLK
    i = pl.program_id(1)

    @pl.when(i == 0)
    def _():
        for j in range(n_blocks):
            kb = k_ref[j * BLK:(j + 1) * BLK, :].astype(F32)
            kmean_ref[j:j + 1, :] = jnp.sum(kb, axis=0, keepdims=True) * (1.0 / BLK)

    blk_id = lax.broadcasted_iota(jnp.int32, (n_blocks, BLK), 0)
    past = blk_id < i
    k_pos = lax.broadcasted_iota(jnp.int32, (BLK, BLK), 0)
    q_pos = lax.broadcasted_iota(jnp.int32, (BLK, BLK), 1)
    causal = k_pos <= q_pos

    def scores_into(slot, j, h):
        rows = pl.ds(pl.multiple_of(jnp.minimum(j, n_blocks - 1) * BLK, BLK), BLK)
        hs = slice(h * hd, (h + 1) * hd)
        s_ref[slot, h] = _dot_nt(k_ref[rows, hs], q_ref[:, hs])

    gates = [_dot_nt(kmean_ref[:, h * hd:(h + 1) * hd].astype(BF16), q_ref[:, h * hd:(h + 1) * hd])
             for h in range(H)]
    for h in range(H):
        scores_into(1, i, h)
    for h in range(H):
        scores_into(0, 0, h)

    m_init = []
    for h in range(H):
        g = jnp.where(past, gates[h], -jnp.inf)
        chosen = jnp.zeros(g.shape, F32)
        for _ in range(MOBA_TOP_K):
            m = jnp.max(g, axis=0, keepdims=True)
            idx = jnp.min(jnp.where(g == m, blk_id, n_blocks), axis=0, keepdims=True)
            hit = blk_id == idx
            chosen = jnp.where(hit, 1.0, chosen)
            g = jnp.where(hit, -jnp.inf, g)
        sel_ref[h] = jnp.where(past, chosen, 0.0)
        s = jnp.where(causal, s_ref[1, h], NEG_BIG)
        m0 = jnp.max(s, axis=0, keepdims=True)
        acc_ref[h] = _dot(vt_ref[h, i], jnp.exp2(s - m0).astype(BF16))
        m_init.append(m0)

    def softmax_of(slot, j, h, m):
        s = s_ref[slot, h]
        picked = sel_ref[h, pl.ds(j, 1), :] > 0.0
        m_new = jnp.maximum(m, jnp.where(picked, jnp.max(s, axis=0, keepdims=True), NEG_BIG))
        p = jnp.exp2(s - jnp.where(picked, m_new, -NEG_BIG)).astype(BF16)
        return p, jnp.exp2(m - m_new), m_new

    def accumulate(j, h, p, alpha):
        acc_ref[h] = alpha * acc_ref[h] + _dot(vt_ref[h, j], p)

    def body(t, ms):
        ms = list(ms)
        blk = (2 * t, 2 * t + 1)

        def produce(slot, h):
            scores_into(slot, blk[1 - slot] + 1, h)

        def consume(slot, h):
            p, alpha, ms[h] = softmax_of(slot, blk[slot], h, ms[h])
            accumulate(blk[slot], h, p, alpha)

        for op in "P10 P11 P12 C00 P13 C01 P00 C02 P01 C03 P02 C10 P03 C11 C12 C13".split():
            (produce if op[0] == "P" else consume)(int(op[1]), int(op[2]))
        return tuple(ms)

    lax.fori_loop(0, (i + 1) // 2, body, tuple(m_init))
    for h in range(H):
        o = acc_ref[h, :hd, :] / acc_ref[h, hd:hd + 1, :]
        o_ref[:, h * hd:(h + 1) * hd] = o.T.astype(o_ref.dtype)


def _moba(pm, vt, batch, seq):
    n = pm.shape[0]
    BLK, hd, H = MOBA_BLOCK, MOBA_HEAD_DIM, MOBA_HEADS
    n_blocks = seq // BLK
    vt_rows = vt.shape[-2]
    return pl.pallas_call(
        _moba_kernel,
        grid=(batch, n_blocks),
        in_specs=[
            pl.BlockSpec((BLK, H * hd), lambda b, i: (b * n_blocks + i, 0)),
            pl.BlockSpec((seq, H * hd), lambda b, i: (b, 1)),
            pl.BlockSpec((None, H, n_blocks, vt_rows, BLK), lambda b, i: (b, 0, 0, 0, 0)),
        ],
        out_specs=pl.BlockSpec((BLK, H * hd), lambda b, i: (b * n_blocks + i, 0)),
        out_shape=jax.ShapeDtypeStruct((n, MOBA_WIDTH), BF16),
        scratch_shapes=[
            pltpu.VMEM((n_blocks, H * hd), F32),
            pltpu.VMEM((H, n_blocks, BLK), F32),
            pltpu.VMEM((H, vt_rows, BLK), F32),
            pltpu.VMEM((2, H, BLK, BLK), F32),
        ],
        compiler_params=pltpu.CompilerParams(
            dimension_semantics=("arbitrary", "arbitrary"),
            vmem_limit_bytes=VMEM_LIMIT_BYTES),
        name="moba",
    )(pm, pm, vt)


def _memkv_kernel(mem_ref, g_ref, wk_ref, wv_ref, k_ref, v_ref):
    h = _rms(mem_ref[...], g_ref[...]).astype(BF16)
    k_ref[...] = _dot(h, wk_ref[...]).astype(BF16)
    v_ref[...] = _dot(h, wv_ref[...]).astype(BF16)


def _memkv(memf, g, wk, wv):
    n = memf.shape[0]
    tm = MEM_LEN
    return pl.pallas_call(
        _memkv_kernel,
        grid=(n // tm,),
        in_specs=[
            pl.BlockSpec((tm, D_MODEL), lambda i: (i, 0)),
            _const_spec((1, D_MODEL)),
            _const_spec((D_MODEL, D_MODEL)),
            _const_spec((D_MODEL, D_MODEL)),
        ],
        out_specs=[pl.BlockSpec((tm, D_MODEL), lambda i: (i, 0))] * 2,
        out_shape=[jax.ShapeDtypeStruct((n, D_MODEL), BF16)] * 2,
        compiler_params=pltpu.CompilerParams(
            dimension_semantics=("arbitrary",), vmem_limit_bytes=VMEM_LIMIT_BYTES),
        name="memkv",
    )(memf, g, wk, wv)


def _mix_xattn_kernel(x_ref, oh_ref, om_ref, wo_ref, g_ref, wq_ref, k_ref, v_ref, wxo_ref, y_ref):
    x1 = (x_ref[...] + _dot(oh_ref[...], wo_ref[:HGRN_WIDTH, :])
          + _dot(om_ref[...], wo_ref[HGRN_WIDTH:, :]))
    h = _rms(x1, g_ref[...]).astype(BF16)
    q = (_dot(h, wq_ref[...]) * (XATTN_HEAD_DIM ** -0.5 * LOG2E)).astype(BF16)
    outs = []
    for hd in range(XATTN_HEADS):
        hs = slice(hd * XATTN_HEAD_DIM, (hd + 1) * XATTN_HEAD_DIM)
        s = _dot_nt(q[:, hs], k_ref[:, hs])
        p = jnp.exp2(s - jnp.max(s, axis=-1, keepdims=True))
        l = jnp.sum(p, axis=-1, keepdims=True)
        outs.append((_dot(p.astype(BF16), v_ref[:, hs]) / l).astype(BF16))
    o = jnp.concatenate(outs, axis=-1)
    y_ref[...] = x1 + _dot(o, wxo_ref[...])


def _mix_xattn(xf, oh, om, wo, g, wq, kmem, vmem, wxo, batch, seq):
    n = xf.shape[0]
    tm = TOKEN_TILE
    steps = seq // tm
    tok = lambda w: pl.BlockSpec((tm, w), lambda b, i: (b * steps + i, 0))
    return pl.pallas_call(
        _mix_xattn_kernel,
        grid=(batch, steps),
        in_specs=[
            tok(D_MODEL), tok(HGRN_WIDTH), tok(MOBA_WIDTH),
            _const_spec((D_MODEL, D_MODEL)),
            _const_spec((1, D_MODEL)),
            _const_spec((D_MODEL, D_MODEL)),
            pl.BlockSpec((MEM_LEN, D_MODEL), lambda b, i: (b, 0)),
            pl.BlockSpec((MEM_LEN, D_MODEL), lambda b, i: (b, 0)),
            _const_spec((D_MODEL, D_MODEL)),
        ],
        out_specs=tok(D_MODEL),
        out_shape=jax.ShapeDtypeStruct((n, D_MODEL), F32),
        compiler_params=pltpu.CompilerParams(
            dimension_semantics=("arbitrary", "arbitrary"), vmem_limit_bytes=VMEM_LIMIT_BYTES),
        name="mix_xattn",
    )(xf, oh, om, wo, g, wq, kmem, vmem, wxo)


def _mlp_kernel(x_ref, g_ref, w1_ref, w2_ref, gf_ref, y_ref, *, final_norm):
    x = x_ref[...]
    h = _rms(x, g_ref[...]).astype(BF16)
    ff_chunk = D_MODEL
    acc = x
    for c in range(D_FF // ff_chunk):
        cs = slice(c * ff_chunk, (c + 1) * ff_chunk)
        z = jnp.maximum(_dot(h, w1_ref[:, cs]), 0.0)
        acc = acc + _dot((z * z).astype(BF16), w2_ref[cs, :])
    y_ref[...] = _rms(acc, gf_ref[...]) if final_norm else acc


def _mlp(xf, g, w1, w2, gf, final_norm):
    n = xf.shape[0]
    tm = TOKEN_TILE
    return pl.pallas_call(
        functools.partial(_mlp_kernel, final_norm=final_norm),
        grid=(n // tm,),
        in_specs=[
            pl.BlockSpec((tm, D_MODEL), lambda i: (i, 0)),
            _const_spec((1, D_MODEL)),
            _const_spec((D_MODEL, D_FF)),
            _const_spec((D_FF, D_MODEL)),
            _const_spec((1, D_MODEL)),
        ],
        out_specs=pl.BlockSpec((tm, D_MODEL), lambda i: (i, 0)),
        out_shape=jax.ShapeDtypeStruct((n, D_MODEL), F32),
        compiler_params=pltpu.CompilerParams(
            dimension_semantics=("arbitrary",), vmem_limit_bytes=VMEM_LIMIT_BYTES),
        name="mlp",
    )(xf, g, w1, w2, gf)


def _rotary_tables(seq):
    half = ROT_DIM // 2
    inv_freq = np.float32(ROPE_THETA) ** (-np.arange(half, dtype=np.float32) * np.float32(2.0 / ROT_DIM))
    ang = (np.arange(seq, dtype=np.float32)[:, None] * inv_freq[None, :]).astype(np.float64)
    cos, sin = np.cos(ang), np.sin(ang)
    rest = MOBA_HEAD_DIM - ROT_DIM
    cos_t = np.concatenate([cos, cos, np.ones((seq, rest))], axis=-1)
    sa_t = np.concatenate([-sin, np.zeros((seq, MOBA_HEAD_DIM - half))], axis=-1)
    sb_t = np.concatenate([np.zeros((seq, half)), sin, np.zeros((seq, rest))], axis=-1)
    return tuple(jnp.asarray(t, F32) for t in (cos_t, sa_t, sb_t))


def _cumsum_matrix():
    T, C = HGRN_TILE, HGRN_CHUNK
    r = np.arange(T)
    same = (r[:, None] // C) == (r[None, :] // C)
    return jnp.asarray(same & (r[None, :] <= r[:, None]), BF16)


def kernel(x, mem, norm_mix, w_in, lb_logits, hgrn_norm, w_out, norm_xattn, norm_mem,
           w_xq, w_xk, w_xv, w_xo, norm_mlp, w_ff1, w_ff2, norm_final):
    batch, seq, _ = x.shape
    n = batch * seq
    xf = x.reshape(n, D_MODEL)
    lb_table = jnp.cumsum(jax.nn.softmax(lb_logits.astype(F32), axis=0), axis=0)
    cos_t, sa_t, sb_t = _rotary_tables(seq)
    cum = _cumsum_matrix()
    row = lambda v: v.reshape(1, -1)
    depth = norm_mix.shape[0]
    for l in range(depth):
        ph, pm, vt = _inproj(xf, row(norm_mix[l]), w_in[l].astype(BF16), cos_t, sa_t, sb_t,
                             batch, seq)
        o_hgrn = _hgrn(ph, row(lb_table[l]), row(hgrn_norm[l]), cum, batch, seq)
        o_moba = _moba(pm, vt, batch, seq)
        kmem, vmem = _memkv(mem.reshape(-1, D_MODEL), row(norm_mem[l]),
                            w_xk[l].astype(BF16), w_xv[l].astype(BF16))
        x2 = _mix_xattn(xf, o_hgrn, o_moba, w_out[l].astype(BF16), row(norm_xattn[l]),
                        w_xq[l].astype(BF16), kmem, vmem, w_xo[l].astype(BF16), batch, seq)
        xf = _mlp(x2, row(norm_mlp[l]), w_ff1[l].astype(BF16), w_ff2[l].astype(BF16),
                  row(norm_final), final_norm=(l == depth - 1))
    return xf.reshape(batch, seq, D_MODEL)
```

```python
import functools
import math

import jax
import jax.numpy as jnp
import numpy as np
from jax import lax
from jax.experimental import pallas as pl
from jax.experimental.pallas import tpu as pltpu

F32 = jnp.float32
BF16 = jnp.bfloat16

D_MODEL = 1024
MEM_LEN = 256
HGRN_HEADS = 4
HGRN_WIDTH = 512
HGRN_HEAD_DIM = 128
HGRN_CHUNK = 32
MOBA_HEADS = 4
MOBA_WIDTH = 512
MOBA_HEAD_DIM = 128
MOBA_BLOCK = 256
MOBA_TOP_K = 3
ROPE_THETA = 500000.0
ROT_DIM = 32
XATTN_HEADS = 4
XATTN_HEAD_DIM = 256
D_FF = 4096
NORM_EPS = 1e-6
IN_COLS = 4 * HGRN_WIDTH + 3 * MOBA_WIDTH

LOG2E = math.log2(math.e)
NEG_BIG = -1e30

VMEM_LIMIT_BYTES = 56 * 1024 * 1024

BF16_SUBLANES = 16
HGRN_TILE = 256
HGRN_SUBTILES = 4
TOKEN_TILE = 1024


def _rms(x, g):
    return x * lax.rsqrt(jnp.mean(x * x, axis=-1, keepdims=True) + NORM_EPS) * g


def _dot(a, b):
    return jnp.dot(a, b, preferred_element_type=F32)


def _dot_nt(a, b):
    return lax.dot_general(a, b, (((1,), (1,)), ((), ())), preferred_element_type=F32)


def _const_spec(shape):
    zeros = (0,) * len(shape)
    return pl.BlockSpec(shape, lambda *_: zeros, pipeline_mode=pl.Buffered(1))


def _inproj_kernel(x_ref, g_ref, w_ref, cos_ref, sa_ref, sb_ref, ph_ref, pm_ref, vt_ref):
    h = _rms(x_ref[...], g_ref[...]).astype(BF16)
    hg_cols = 4 * HGRN_WIDTH
    hd, BLK = MOBA_HEAD_DIM, MOBA_BLOCK
    ph_ref[...] = _dot(h, w_ref[:, :hg_cols])
    qk = _dot(h, w_ref[:, hg_cols:hg_cols + 2 * MOBA_WIDTH])
    cos, sa, sb = cos_ref[...], sa_ref[...], sb_ref[...]
    q_scale = hd ** -0.5 * LOG2E
    for o in range(0, 2 * MOBA_WIDTH, hd):
        t = qk[:, o:o + hd]
        r = t * cos + pltpu.roll(t, hd - ROT_DIM // 2, 1) * sa + pltpu.roll(t, ROT_DIM // 2, 1) * sb
        if o < MOBA_WIDTH:
            r = r * q_scale
        pm_ref[:, o:o + hd] = r.astype(BF16)
    v = _dot(h, w_ref[:, hg_cols + 2 * MOBA_WIDTH:])
    ones_rows = jnp.ones((vt_ref.shape[2] - hd, BLK), BF16)
    for head in range(MOBA_HEADS):
        for blk in range(v.shape[0] // BLK):
            vt_ref[head, blk, :hd, :] = v[blk * BLK:(blk + 1) * BLK, head * hd:(head + 1) * hd].T.astype(BF16)
            vt_ref[head, blk, hd:, :] = ones_rows


def _inproj(xf, g, w, cos_t, sa_t, sb_t, batch, seq):
    n = xf.shape[0]
    tm = TOKEN_TILE
    tiles_per_seq = seq // tm
    blocks_per_tile = tm // MOBA_BLOCK
    vt_rows = MOBA_HEAD_DIM + BF16_SUBLANES
    rot = lambda: pl.BlockSpec((tm, MOBA_HEAD_DIM), lambda i: (i % tiles_per_seq, 0))
    return pl.pallas_call(
        _inproj_kernel,
        grid=(n // tm,),
        in_specs=[
            pl.BlockSpec((tm, D_MODEL), lambda i: (i, 0)),
            _const_spec((1, D_MODEL)),
            _const_spec((D_MODEL, IN_COLS)),
            rot(), rot(), rot(),
        ],
        out_specs=[
            pl.BlockSpec((tm, 4 * HGRN_WIDTH), lambda i: (i, 0)),
            pl.BlockSpec((tm, 2 * MOBA_WIDTH), lambda i: (i, 0)),
            pl.BlockSpec((None, MOBA_HEADS, blocks_per_tile, vt_rows, MOBA_BLOCK),
                         lambda i: (i // tiles_per_seq, 0, i % tiles_per_seq, 0, 0)),
        ],
        out_shape=[
            jax.ShapeDtypeStruct((n, 4 * HGRN_WIDTH), F32),
            jax.ShapeDtypeStruct((n, 2 * MOBA_WIDTH), BF16),
            jax.ShapeDtypeStruct((batch, MOBA_HEADS, seq // MOBA_BLOCK, vt_rows, MOBA_BLOCK), BF16),
        ],
        compiler_params=pltpu.CompilerParams(
            dimension_semantics=("arbitrary",), vmem_limit_bytes=VMEM_LIMIT_BYTES),
        name="inproj",
    )(xf, g, w, cos_t, sa_t, sb_t)


def _hgrn_kernel(ph_ref, lb_ref, ng_ref, cum_ref, o_ref,
                 st_ref, b_ref, kk_ref, qt_ref, kt_ref, keb_ref, dec_ref, oi_ref):
    T, W, C, dh = HGRN_TILE, HGRN_WIDTH, HGRN_CHUNK, HGRN_HEAD_DIM
    n_chunks = T // C

    @pl.when((pl.program_id(0) == 0) & (pl.program_id(1) == 0))
    def _():
        keb_ref[...] = jnp.zeros_like(keb_ref)

    @pl.when(pl.program_id(1) == 0)
    def _():
        st_ref[...] = jnp.zeros_like(st_ref)

    lb = lb_ref[...]
    row = lax.broadcasted_iota(jnp.int32, (T, T), 0)
    col = lax.broadcasted_iota(jnp.int32, (T, T), 1)
    intra_mask = (row // C == col // C) & (col <= row)
    cum = cum_ref[...]

    for u in range(HGRN_SUBTILES):
        rows = slice(u * T, (u + 1) * T)
        f = lb + (1.0 - lb) * jax.nn.sigmoid(ph_ref[rows, W:2 * W])
        logf = jnp.log2(f)
        kk_ref[u] = 1.0 - f
        l1 = logf.astype(BF16)
        l2 = (logf - l1.astype(F32)).astype(BF16)
        b_ref[u] = _dot(cum, l1) + _dot(cum, l2)

    for u in range(HGRN_SUBTILES):
        rows = slice(u * T, (u + 1) * T)
        for h in range(HGRN_HEADS):
            hs = slice(h * dh, (h + 1) * dh)
            b = b_ref[u, :, hs]
            qt_ref[u, :, hs] = (ph_ref[rows, h * dh:(h + 1) * dh] * jnp.exp2(b)).astype(BF16)
            kt_ref[u, :, hs] = (kk_ref[u, :, hs] * jnp.exp2(-b)).astype(BF16)
            for c in range(n_chunks):
                cs = slice(c * C, (c + 1) * C)
                b_last = b_ref[u, (c + 1) * C - 1:(c + 1) * C, hs]
                dec_ref[u, c:c + 1, hs] = jnp.exp2(b_last)
                keb_ref[u, h, cs, c * dh:(c + 1) * dh] = (
                    kk_ref[u, cs, hs] * jnp.exp2(b_last - b_ref[u, cs, hs])).astype(BF16)
            vf = ph_ref[rows, 2 * W + h * dh:2 * W + (h + 1) * dh]
            a = jnp.where(intra_mask, _dot_nt(qt_ref[u, :, hs], kt_ref[u, :, hs]), 0.0)
            upd = _dot(vf.T.astype(BF16), keb_ref[u, h])
            oi_ref[u, h] = _dot(a.astype(BF16), vf.astype(BF16))
            st = st_ref[h]
            for c in range(n_chunks):
                cs = slice(c * C, (c + 1) * C)
                oi_ref[u, h, cs, :] += _dot_nt(qt_ref[u, cs, hs], st.astype(BF16))
                st = st * dec_ref[u, c:c + 1, hs] + upd[:, c * dh:(c + 1) * dh]
            st_ref[h] = st
            o_h = oi_ref[u, h]
            o_h = o_h * lax.rsqrt(jnp.mean(o_h * o_h, axis=-1, keepdims=True) + NORM_EPS)
            g = ph_ref[rows, 3 * W + h * dh:3 * W + (h + 1) * dh]
            o_h = o_h * ng_ref[:, hs] * (g * jax.nn.sigmoid(g))
            o_ref[rows, hs] = o_h.astype(o_ref.dtype)


def _hgrn(ph, lb, ng, cum, batch, seq):
    n = ph.shape[0]
    T, U = HGRN_TILE, HGRN_SUBTILES
    steps = seq // (T * U)
    n_chunks = T // HGRN_CHUNK
    return pl.pallas_call(
        _hgrn_kernel,
        grid=(batch, steps),
        in_specs=[
            pl.BlockSpec((T * U, 4 * HGRN_WIDTH), lambda b, i: (b * steps + i, 0)),
            _const_spec((1, HGRN_WIDTH)),
            _const_spec((1, HGRN_WIDTH)),
            _const_spec((T, T)),
        ],
        out_specs=pl.BlockSpec((T * U, HGRN_WIDTH), lambda b, i: (b * steps + i, 0)),
        out_shape=jax.ShapeDtypeStruct((n, HGRN_WIDTH), BF16),
        scratch_shapes=[
            pltpu.VMEM((HGRN_HEADS, HGRN_HEAD_DIM, HGRN_HEAD_DIM), F32),
            pltpu.VMEM((U, T, HGRN_WIDTH), F32),
            pltpu.VMEM((U, T, HGRN_WIDTH), F32),
            pltpu.VMEM((U, T, HGRN_WIDTH), BF16),
            pltpu.VMEM((U, T, HGRN_WIDTH), BF16),
            pltpu.VMEM((U, HGRN_HEADS, T, n_chunks * HGRN_HEAD_DIM), BF16),
            pltpu.VMEM((U, n_chunks, HGRN_WIDTH), F32),
            pltpu.VMEM((U, HGRN_HEADS, T, HGRN_HEAD_DIM), F32),
        ],
        compiler_params=pltpu.CompilerParams(
            dimension_semantics=("arbitrary", "arbitrary"), vmem_limit_bytes=VMEM_LIMIT_BYTES),
        name="hgrn",
    )(ph, lb, ng, cum)


def _moba_kernel(q_ref, k_ref, vt_ref, o_ref, kmean_ref, sel_ref, acc_ref, s_ref):
    BLK, hd, H = MOBA_BLOCK, MOBA_HEAD_DIM, MOBA_HEADS
    n_blocks = k_ref.shape[0] // BLK
    i = pl.program_id(1)

    @pl.when(i == 0)
    def _():
        for j in range(n_blocks):
            kb = k_ref[j * BLK:(j + 1) * BLK, :].astype(F32)
            kmean_ref[j:j + 1, :] = jnp.sum(kb, axis=0, keepdims=True) * (1.0 / BLK)

    blk_id = lax.broadcasted_iota(jnp.int32, (n_blocks, BLK), 0)
    past = blk_id < i
    k_pos = lax.broadcasted_iota(jnp.int32, (BLK, BLK), 0)
    q_pos = lax.broadcasted_iota(jnp.int32, (BLK, BLK), 1)
    causal = k_pos <= q_pos

    def scores_into(slot, j, h):
        rows = pl.ds(pl.multiple_of(jnp.minimum(j, n_blocks - 1) * BLK, BLK), BLK)
        hs = slice(h * hd, (h + 1) * hd)
        s_ref[slot, h] = _dot_nt(k_ref[rows, hs], q_ref[:, hs])

    gates = [_dot_nt(kmean_ref[:, h * hd:(h + 1) * hd].astype(BF16), q_ref[:, h * hd:(h + 1) * hd])
             for h in range(H)]
    for h in range(H):
        scores_into(1, i, h)
    for h in range(H):
        scores_into(0, 0, h)

    m_init = []
    for h in range(H):
        g = jnp.where(past, gates[h], -jnp.inf)
        chosen = jnp.zeros(g.shape, F32)
        for _ in range(MOBA_TOP_K):
            m = jnp.max(g, axis=0, keepdims=True)
            idx = jnp.min(jnp.where(g == m, blk_id, n_blocks), axis=0, keepdims=True)
            hit = blk_id == idx
            chosen = jnp.where(hit, 1.0, chosen)
            g = jnp.where(hit, -jnp.inf, g)
        sel_ref[h] = jnp.where(past, chosen, 0.0)
        s = jnp.where(causal, s_ref[1, h], NEG_BIG)
        m0 = jnp.max(s, axis=0, keepdims=True)
        acc_ref[h] = _dot(vt_ref[h, i], jnp.exp2(s - m0).astype(BF16))
        m_init.append(m0)

    def softmax_of(slot, j, h, m):
        s = s_ref[slot, h]
        picked = sel_ref[h, pl.ds(j, 1), :] > 0.0
        m_new = jnp.maximum(m, jnp.where(picked, jnp.max(s, axis=0, keepdims=True), NEG_BIG))
        p = jnp.exp2(s - jnp.where(picked, m_new, -NEG_BIG)).astype(BF16)
        return p, jnp.exp2(m - m_new), m_new

    def accumulate(j, h, p, alpha):
        acc_ref[h] = alpha * acc_ref[h] + _dot(vt_ref[h, j], p)

    def body(t, ms):
        ms = list(ms)
        blk = (2 * t, 2 * t + 1)

        def produce(slot, h):
            scores_into(slot, blk[1 - slot] + 1, h)

        def consume(slot, h):
            p, alpha, ms[h] = softmax_of(slot, blk[slot], h, ms[h])
            accumulate(blk[slot], h, p, alpha)

        for op in "P10 P11 P12 C00 P13 C01 P00 C02 P01 C03 P02 C10 P03 C11 C12 C13".split():
            (produce if op[0] == "P" else consume)(int(op[1]), int(op[2]))
        return tuple(ms)

    lax.fori_loop(0, (i + 1) // 2, body, tuple(m_init))
    for h in range(H):
        o = acc_ref[h, :hd, :] / acc_ref[h, hd:hd + 1, :]
        o_ref[:, h * hd:(h + 1) * hd] = o.T.astype(o_ref.dtype)


def _moba(pm, vt, batch, seq):
    n = pm.shape[0]
    BLK, hd, H = MOBA_BLOCK, MOBA_HEAD_DIM, MOBA_HEADS
    n_blocks = seq // BLK
    vt_rows = vt.shape[-2]
    return pl.pallas_call(
        _moba_kernel,
        grid=(batch, n_blocks),
        in_specs=[
            pl.BlockSpec((BLK, H * hd), lambda b, i: (b * n_blocks + i, 0)),
            pl.BlockSpec((seq, H * hd), lambda b, i: (b, 1)),
            pl.BlockSpec((None, H, n_blocks, vt_rows, BLK), lambda b, i: (b, 0, 0, 0, 0)),
        ],
        out_specs=pl.BlockSpec((BLK, H * hd), lambda b, i: (b * n_blocks + i, 0)),
        out_shape=jax.ShapeDtypeStruct((n, MOBA_WIDTH), BF16),
        scratch_shapes=[
            pltpu.VMEM((n_blocks, H * hd), F32),
            pltpu.VMEM((H, n_blocks, BLK), F32),
            pltpu.VMEM((H, vt_rows, BLK), F32),
            pltpu.VMEM((2, H, BLK, BLK), F32),
        ],
        compiler_params=pltpu.CompilerParams(
            dimension_semantics=("arbitrary", "arbitrary"),
            vmem_limit_bytes=VMEM_LIMIT_BYTES),
        name="moba",
    )(pm, pm, vt)


def _memkv_kernel(mem_ref, g_ref, wk_ref, wv_ref, k_ref, v_ref):
    h = _rms(mem_ref[...], g_ref[...]).astype(BF16)
    k_ref[...] = _dot(h, wk_ref[...]).astype(BF16)
    v_ref[...] = _dot(h, wv_ref[...]).astype(BF16)


def _memkv(memf, g, wk, wv):
    n = memf.shape[0]
    tm = MEM_LEN
    return pl.pallas_call(
        _memkv_kernel,
        grid=(n // tm,),
        in_specs=[
            pl.BlockSpec((tm, D_MODEL), lambda i: (i, 0)),
            _const_spec((1, D_MODEL)),
            _const_spec((D_MODEL, D_MODEL)),
            _const_spec((D_MODEL, D_MODEL)),
        ],
        out_specs=[pl.BlockSpec((tm, D_MODEL), lambda i: (i, 0))] * 2,
        out_shape=[jax.ShapeDtypeStruct((n, D_MODEL), BF16)] * 2,
        compiler_params=pltpu.CompilerParams(
            dimension_semantics=("arbitrary",), vmem_limit_bytes=VMEM_LIMIT_BYTES),
        name="memkv",
    )(memf, g, wk, wv)


def _mix_xattn_kernel(x_ref, oh_ref, om_ref, wo_ref, g_ref, wq_ref, k_ref, v_ref, wxo_ref, y_ref):
    x1 = (x_ref[...] + _dot(oh_ref[...], wo_ref[:HGRN_WIDTH, :])
          + _dot(om_ref[...], wo_ref[HGRN_WIDTH:, :]))
    h = _rms(x1, g_ref[...]).astype(BF16)
    q = (_dot(h, wq_ref[...]) * (XATTN_HEAD_DIM ** -0.5 * LOG2E)).astype(BF16)
    outs = []
    for hd in range(XATTN_HEADS):
        hs = slice(hd * XATTN_HEAD_DIM, (hd + 1) * XATTN_HEAD_DIM)
        s = _dot_nt(q[:, hs], k_ref[:, hs])
        p = jnp.exp2(s - jnp.max(s, axis=-1, keepdims=True))
        l = jnp.sum(p, axis=-1, keepdims=True)
        outs.append((_dot(p.astype(BF16), v_ref[:, hs]) / l).astype(BF16))
    o = jnp.concatenate(outs, axis=-1)
    y_ref[...] = x1 + _dot(o, wxo_ref[...])


def _mix_xattn(xf, oh, om, wo, g, wq, kmem, vmem, wxo, batch, seq):
    n = xf.shape[0]
    tm = TOKEN_TILE
    steps = seq // tm
    tok = lambda w: pl.BlockSpec((tm, w), lambda b, i: (b * steps + i, 0))
    return pl.pallas_call(
        _mix_xattn_kernel,
        grid=(batch, steps),
        in_specs=[
            tok(D_MODEL), tok(HGRN_WIDTH), tok(MOBA_WIDTH),
            _const_spec((D_MODEL, D_MODEL)),
            _const_spec((1, D_MODEL)),
            _const_spec((D_MODEL, D_MODEL)),
            pl.BlockSpec((MEM_LEN, D_MODEL), lambda b, i: (b, 0)),
            pl.BlockSpec((MEM_LEN, D_MODEL), lambda b, i: (b, 0)),
            _const_spec((D_MODEL, D_MODEL)),
        ],
        out_specs=tok(D_MODEL),
        out_shape=jax.ShapeDtypeStruct((n, D_MODEL), F32),
        compiler_params=pltpu.CompilerParams(
            dimension_semantics=("arbitrary", "arbitrary"), vmem_limit_bytes=VMEM_LIMIT_BYTES),
        name="mix_xattn",
    )(xf, oh, om, wo, g, wq, kmem, vmem, wxo)


def _mlp_kernel(x_ref, g_ref, w1_ref, w2_ref, gf_ref, y_ref, *, final_norm):
    x = x_ref[...]
    h = _rms(x, g_ref[...]).astype(BF16)
    ff_chunk = D_MODEL
    acc = x
    for c in range(D_FF // ff_chunk):
        cs = slice(c * ff_chunk, (c + 1) * ff_chunk)
        z = jnp.maximum(_dot(h, w1_ref[:, cs]), 0.0)
        acc = acc + _dot((z * z).astype(BF16), w2_ref[cs, :])
    y_ref[...] = _rms(acc, gf_ref[...]) if final_norm else acc


def _mlp(xf, g, w1, w2, gf, final_norm):
    n = xf.shape[0]
    tm = TOKEN_TILE
    return pl.pallas_call(
        functools.partial(_mlp_kernel, final_norm=final_norm),
        grid=(n // tm,),
        in_specs=[
            pl.BlockSpec((tm, D_MODEL), lambda i: (i, 0)),
            _const_spec((1, D_MODEL)),
            _const_spec((D_MODEL, D_FF)),
            _const_spec((D_FF, D_MODEL)),
            _const_spec((1, D_MODEL)),
        ],
        out_specs=pl.BlockSpec((tm, D_MODEL), lambda i: (i, 0)),
        out_shape=jax.ShapeDtypeStruct((n, D_MODEL), F32),
        compiler_params=pltpu.CompilerParams(
            dimension_semantics=("arbitrary",), vmem_limit_bytes=VMEM_LIMIT_BYTES),
        name="mlp",
    )(xf, g, w1, w2, gf)


def _rotary_tables(seq):
    half = ROT_DIM // 2
    inv_freq = np.float32(ROPE_THETA) ** (-np.arange(half, dtype=np.float32) * np.float32(2.0 / ROT_DIM))
    ang = (np.arange(seq, dtype=np.float32)[:, None] * inv_freq[None, :]).astype(np.float64)
    cos, sin = np.cos(ang), np.sin(ang)
    rest = MOBA_HEAD_DIM - ROT_DIM
    cos_t = np.concatenate([cos, cos, np.ones((seq, rest))], axis=-1)
    sa_t = np.concatenate([-sin, np.zeros((seq, MOBA_HEAD_DIM - half))], axis=-1)
    sb_t = np.concatenate([np.zeros((seq, half)), sin, np.zeros((seq, rest))], axis=-1)
    return tuple(jnp.asarray(t, F32) for t in (cos_t, sa_t, sb_t))


def _cumsum_matrix():
    T, C = HGRN_TILE, HGRN_CHUNK
    r = np.arange(T)
    same = (r[:, None] // C) == (r[None, :] // C)
    return jnp.asarray(same & (r[None, :] <= r[:, None]), BF16)


def kernel(x, mem, norm_mix, w_in, lb_logits, hgrn_norm, w_out, norm_xattn, norm_mem,
           w_xq, w_xk, w_xv, w_xo, norm_mlp, w_ff1, w_ff2, norm_final):
    batch, seq, _ = x.shape
    n = batch * seq
    xf = x.reshape(n, D_MODEL)
    lb_table = jnp.cumsum(jax.nn.softmax(lb_logits.astype(F32), axis=0), axis=0)
    cos_t, sa_t, sb_t = _rotary_tables(seq)
    cum = _cumsum_matrix()
    row = lambda v: v.reshape(1, -1)
    depth = norm_mix.shape[0]
    for l in range(depth):
        ph, pm, vt = _inproj(xf, row(norm_mix[l]), w_in[l].astype(BF16), cos_t, sa_t, sb_t,
                             batch, seq)
        o_hgrn = _hgrn(ph, row(lb_table[l]), row(hgrn_norm[l]), cum, batch, seq)
        o_moba = _moba(pm, vt, batch, seq)
        kmem, vmem = _memkv(mem.reshape(-1, D_MODEL), row(norm_mem[l]),
                            w_xk[l].astype(BF16), w_xv[l].astype(BF16))
        x2 = _mix_xattn(xf, o_hgrn, o_moba, w_out[l].astype(BF16), row(norm_xattn[l]),
                        w_xq[l].astype(BF16), kmem, vmem, w_xo[l].astype(BF16), batch, seq)
        xf = _mlp(x2, row(norm_mlp[l]), w_ff1[l].astype(BF16), w_ff2[l].astype(BF16),
                  row(norm_final), final_norm=(l == depth - 1))
    return xf.reshape(batch, seq, D_MODEL)
```

```python
import functools
import math

import jax
import jax.numpy as jnp
import numpy as np
from jax import lax
from jax.experimental import pallas as pl
from jax.experimental.pallas import tpu as pltpu

F32 = jnp.float32
BF16 = jnp.bfloat16

D_MODEL = 1024
MEM_LEN = 256
HGRN_HEADS = 4
HGRN_WIDTH = 512
HGRN_HEAD_DIM = 128
HGRN_CHUNK = 32
MOBA_HEADS = 4
MOBA_WIDTH = 512
MOBA_HEAD_DIM = 128
MOBA_BLOCK = 256
MOBA_TOP_K = 3
ROPE_THETA = 500000.0
ROT_DIM = 32
XATTN_HEADS = 4
XATTN_HEAD_DIM = 256
D_FF = 4096
NORM_EPS = 1e-6
IN_COLS = 4 * HGRN_WIDTH + 3 * MOBA_WIDTH

LOG2E = math.log2(math.e)
NEG_BIG = -1e30

VMEM_LIMIT_BYTES = 56 * 1024 * 1024

BF16_SUBLANES = 16
HGRN_TILE = 256
HGRN_SUBTILES = 4
TOKEN_TILE = 1024


def _rms(x, g):
    return x * lax.rsqrt(jnp.mean(x * x, axis=-1, keepdims=True) + NORM_EPS) * g


def _dot(a, b):
    return jnp.dot(a, b, preferred_element_type=F32)


def _dot_nt(a, b):
    return lax.dot_general(a, b, (((1,), (1,)), ((), ())), preferred_element_type=F32)


def _const_spec(shape):
    zeros = (0,) * len(shape)
    return pl.BlockSpec(shape, lambda *_: zeros, pipeline_mode=pl.Buffered(1))


def _inproj_kernel(x_ref, g_ref, w_ref, cos_ref, sa_ref, sb_ref, ph_ref, pm_ref, vt_ref):
    h = _rms(x_ref[...], g_ref[...]).astype(BF16)
    hg_cols = 4 * HGRN_WIDTH
    hd, BLK = MOBA_HEAD_DIM, MOBA_BLOCK
    ph_ref[...] = _dot(h, w_ref[:, :hg_cols])
    qk = _dot(h, w_ref[:, hg_cols:hg_cols + 2 * MOBA_WIDTH])
    cos, sa, sb = cos_ref[...], sa_ref[...], sb_ref[...]
    q_scale = hd ** -0.5 * LOG2E
    for o in range(0, 2 * MOBA_WIDTH, hd):
        t = qk[:, o:o + hd]
        r = t * cos + pltpu.roll(t, hd - ROT_DIM // 2, 1) * sa + pltpu.roll(t, ROT_DIM // 2, 1) * sb
        if o < MOBA_WIDTH:
            r = r * q_scale
        pm_ref[:, o:o + hd] = r.astype(BF16)
    v = _dot(h, w_ref[:, hg_cols + 2 * MOBA_WIDTH:])
    ones_rows = jnp.ones((vt_ref.shape[2] - hd, BLK), BF16)
    for head in range(MOBA_HEADS):
        for blk in range(v.shape[0] // BLK):
            vt_ref[head, blk, :hd, :] = v[blk * BLK:(blk + 1) * BLK, head * hd:(head + 1) * hd].T.astype(BF16)
            vt_ref[head, blk, hd:, :] = ones_rows


def _inproj(xf, g, w, cos_t, sa_t, sb_t, batch, seq):
    n = xf.shape[0]
    tm = TOKEN_TILE
    tiles_per_seq = seq // tm
    blocks_per_tile = tm // MOBA_BLOCK
    vt_rows = MOBA_HEAD_DIM + BF16_SUBLANES
    rot = lambda: pl.BlockSpec((tm, MOBA_HEAD_DIM), lambda i: (i % tiles_per_seq, 0))
    return pl.pallas_call(
        _inproj_kernel,
        grid=(n // tm,),
        in_specs=[
            pl.BlockSpec((tm, D_MODEL), lambda i: (i, 0)),
            _const_spec((1, D_MODEL)),
            _const_spec((D_MODEL, IN_COLS)),
            rot(), rot(), rot(),
        ],
        out_specs=[
            pl.BlockSpec((tm, 4 * HGRN_WIDTH), lambda i: (i, 0)),
            pl.BlockSpec((tm, 2 * MOBA_WIDTH), lambda i: (i, 0)),
            pl.BlockSpec((None, MOBA_HEADS, blocks_per_tile, vt_rows, MOBA_BLOCK),
                         lambda i: (i // tiles_per_seq, 0, i % tiles_per_seq, 0, 0)),
        ],
        out_shape=[
            jax.ShapeDtypeStruct((n, 4 * HGRN_WIDTH), F32),
            jax.ShapeDtypeStruct((n, 2 * MOBA_WIDTH), BF16),
            jax.ShapeDtypeStruct((batch, MOBA_HEADS, seq // MOBA_BLOCK, vt_rows, MOBA_BLOCK), BF16),
        ],
        compiler_params=pltpu.CompilerParams(
            dimension_semantics=("arbitrary",), vmem_limit_bytes=VMEM_LIMIT_BYTES),
        name="inproj",
    )(xf, g, w, cos_t, sa_t, sb_t)


def _hgrn_kernel(ph_ref, lb_ref, ng_ref, cum_ref, o_ref,
                 st_ref, b_ref, kk_ref, qt_ref, kt_ref, keb_ref, dec_ref, oi_ref):
    T, W, C, dh = HGRN_TILE, HGRN_WIDTH, HGRN_CHUNK, HGRN_HEAD_DIM
    n_chunks = T // C

    @pl.when((pl.program_id(0) == 0) & (pl.program_id(1) == 0))
    def _():
        keb_ref[...] = jnp.zeros_like(keb_ref)

    @pl.when(pl.program_id(1) == 0)
    def _():
        st_ref[...] = jnp.zeros_like(st_ref)

    lb = lb_ref[...]
    row = lax.broadcasted_iota(jnp.int32, (T, T), 0)
    col = lax.broadcasted_iota(jnp.int32, (T, T), 1)
    intra_mask = (row // C == col // C) & (col <= row)
    cum = cum_ref[...]

    for u in range(HGRN_SUBTILES):
        rows = slice(u * T, (u + 1) * T)
        f = lb + (1.0 - lb) * jax.nn.sigmoid(ph_ref[rows, W:2 * W])
        logf = jnp.log2(f)
        kk_ref[u] = 1.0 - f
        l1 = logf.astype(BF16)
        l2 = (logf - l1.astype(F32)).astype(BF16)
        b_ref[u] = _dot(cum, l1) + _dot(cum, l2)

    for u in range(HGRN_SUBTILES):
        rows = slice(u * T, (u + 1) * T)
        for h in range(HGRN_HEADS):
            hs = slice(h * dh, (h + 1) * dh)
            b = b_ref[u, :, hs]
            qt_ref[u, :, hs] = (ph_ref[rows, h * dh:(h + 1) * dh] * jnp.exp2(b)).astype(BF16)
            kt_ref[u, :, hs] = (kk_ref[u, :, hs] * jnp.exp2(-b)).astype(BF16)
            for c in range(n_chunks):
                cs = slice(c * C, (c + 1) * C)
                b_last = b_ref[u, (c + 1) * C - 1:(c + 1) * C, hs]
                dec_ref[u, c:c + 1, hs] = jnp.exp2(b_last)
                keb_ref[u, h, cs, c * dh:(c + 1) * dh] = (
                    kk_ref[u, cs, hs] * jnp.exp2(b_last - b_ref[u, cs, hs])).astype(BF16)
            vf = ph_ref[rows, 2 * W + h * dh:2 * W + (h + 1) * dh]
            a = jnp.where(intra_mask, _dot_nt(qt_ref[u, :, hs], kt_ref[u, :, hs]), 0.0)
            upd = _dot(vf.T.astype(BF16), keb_ref[u, h])
            oi_ref[u, h] = _dot(a.astype(BF16), vf.astype(BF16))
            st = st_ref[h]
            for c in range(n_chunks):
                cs = slice(c * C, (c + 1) * C)
                oi_ref[u, h, cs, :] += _dot_nt(qt_ref[u, cs, hs], st.astype(BF16))
                st = st * dec_ref[u, c:c + 1, hs] + upd[:, c * dh:(c + 1) * dh]
            st_ref[h] = st
            o_h = oi_ref[u, h]
            o_h = o_h * lax.rsqrt(jnp.mean(o_h * o_h, axis=-1, keepdims=True) + NORM_EPS)
            g = ph_ref[rows, 3 * W + h * dh:3 * W + (h + 1) * dh]
            o_h = o_h * ng_ref[:, hs] * (g * jax.nn.sigmoid(g))
            o_ref[rows, hs] = o_h.astype(o_ref.dtype)


def _hgrn(ph, lb, ng, cum, batch, seq):
    n = ph.shape[0]
    T, U = HGRN_TILE, HGRN_SUBTILES
    steps = seq // (T * U)
    n_chunks = T // HGRN_CHUNK
    return pl.pallas_call(
        _hgrn_kernel,
        grid=(batch, steps),
        in_specs=[
            pl.BlockSpec((T * U, 4 * HGRN_WIDTH), lambda b, i: (b * steps + i, 0)),
            _const_spec((1, HGRN_WIDTH)),
            _const_spec((1, HGRN_WIDTH)),
            _const_spec((T, T)),
        ],
        out_specs=pl.BlockSpec((T * U, HGRN_WIDTH), lambda b, i: (b * steps + i, 0)),
        out_shape=jax.ShapeDtypeStruct((n, HGRN_WIDTH), BF16),
        scratch_shapes=[
            pltpu.VMEM((HGRN_HEADS, HGRN_HEAD_DIM, HGRN_HEAD_DIM), F32),
            pltpu.VMEM((U, T, HGRN_WIDTH), F32),
            pltpu.VMEM((U, T, HGRN_WIDTH), F32),
            pltpu.VMEM((U, T, HGRN_WIDTH), BF16),
            pltpu.VMEM((U, T, HGRN_WIDTH), BF16),
            pltpu.VMEM((U, HGRN_HEADS, T, n_chunks * HGRN_HEAD_DIM), BF16),
            pltpu.VMEM((U, n_chunks, HGRN_WIDTH), F32),
            pltpu.VMEM((U, HGRN_HEADS, T, HGRN_HEAD_DIM), F32),
        ],
        compiler_params=pltpu.CompilerParams(
            dimension_semantics=("arbitrary", "arbitrary"), vmem_limit_bytes=VMEM_LIMIT_BYTES),
        name="hgrn",
    )(ph, lb, ng, cum)


def _moba_kernel(q_ref, k_ref, vt_ref, o_ref, kmean_ref, sel_ref, acc_ref, s_ref):
    BLK, hd, H = MOBA_BLOCK, MOBA_HEAD_DIM, MOBA_HEADS
    n_blocks = k_ref.shape[0] // BLK
    i = pl.program_id(1)

    @pl.when(i == 0)
    def _():
        for j in range(n_blocks):
            kb = k_ref[j * BLK:(j + 1) * BLK, :].astype(F32)
            kmean_ref[j:j + 1, :] = jnp.sum(kb, axis=0, keepdims=True) * (1.0 / BLK)

    blk_id = lax.broadcasted_iota(jnp.int32, (n_blocks, BLK), 0)
    past = blk_id < i
    k_pos = lax.broadcasted_iota(jnp.int32, (BLK, BLK), 0)
    q_pos = lax.broadcasted_iota(jnp.int32, (BLK, BLK), 1)
    causal = k_pos <= q_pos

    def scores_into(slot, j, h):
        rows = pl.ds(pl.multiple_of(jnp.minimum(j, n_blocks - 1) * BLK, BLK), BLK)
        hs = slice(h * hd, (h + 1) * hd)
        s_ref[slot, h] = _dot_nt(k_ref[rows, hs], q_ref[:, hs])

    SLOT_OWN = 4
    PAIR_A, PAIR_B = (0, 1), (2, 3)
    gates = [_dot_nt(kmean_ref[:, h * hd:(h + 1) * hd].astype(BF16), q_ref[:, h * hd:(h + 1) * hd])
             for h in range(H)]
    for h in range(H):
        scores_into(SLOT_OWN, i, h)
    for h in range(H):
        scores_into(PAIR_A[0], 0, h)
        scores_into(PAIR_A[1], 1, h)

    m_init = []
    for h in range(H):
        g = jnp.where(past, gates[h], -jnp.inf)
        chosen = jnp.zeros(g.shape, F32)
        for _ in range(MOBA_TOP_K):
            m = jnp.max(g, axis=0, keepdims=True)
            idx = jnp.min(jnp.where(g == m, blk_id, n_blocks), axis=0, keepdims=True)
            hit = blk_id == idx
            chosen = jnp.where(hit, 1.0, chosen)
            g = jnp.where(hit, -jnp.inf, g)
        sel_ref[h] = jnp.where(past, chosen, 0.0)
        s = jnp.where(causal, s_ref[SLOT_OWN, h], NEG_BIG)
        m0 = jnp.max(s, axis=0, keepdims=True)
        acc_ref[h] = _dot(vt_ref[h, i], jnp.exp2(s - m0).astype(BF16))
        m_init.append(m0)

    def consume_pair(slots, j, h, m):
        sx, sy = s_ref[slots[0], h], s_ref[slots[1], h]
        px = sel_ref[h, pl.ds(j, 1), :] > 0.0
        py = sel_ref[h, pl.ds(j + 1, 1), :] > 0.0
        m_new = jnp.maximum(m, jnp.maximum(
            jnp.where(px, jnp.max(sx, axis=0, keepdims=True), NEG_BIG),
            jnp.where(py, jnp.max(sy, axis=0, keepdims=True), NEG_BIG)))
        ex = jnp.exp2(sx - jnp.where(px, m_new, -NEG_BIG)).astype(BF16)
        ey = jnp.exp2(sy - jnp.where(py, m_new, -NEG_BIG)).astype(BF16)
        acc_ref[h] = (jnp.exp2(m - m_new) * acc_ref[h]
                      + (_dot(vt_ref[h, j], ex) + _dot(vt_ref[h, j + 1], ey)))
        return m_new

    def body(t, ms):
        ms = list(ms)
        base = 4 * t

        def produce(pair, h):
            j = base + 2 if pair == "B" else base + 4
            slots = PAIR_B if pair == "B" else PAIR_A
            scores_into(slots[0], j, h)
            scores_into(slots[1], j + 1, h)

        def consume(pair, h):
            j = base if pair == "A" else base + 2
            ms[h] = consume_pair(PAIR_A if pair == "A" else PAIR_B, j, h, ms[h])

        for op in "PB0 PB1 CA0 PB2 CA1 PB3 CA2 PA0 CA3 PA1 CB0 PA2 CB1 PA3 CB2 CB3".split():
            (produce if op[0] == "P" else consume)(op[1], int(op[2]))
        return tuple(ms)

    n_pairs = (i + 1) // 2
    ms = lax.fori_loop(0, n_pairs // 2, body, tuple(m_init))

    @pl.when(n_pairs % 2 == 1)
    def _():
        for h in range(H):
            consume_pair(PAIR_A, 2 * (n_pairs - 1), h, ms[h])

    for h in range(H):
        o = acc_ref[h, :hd, :] / acc_ref[h, hd:hd + 1, :]
        o_ref[:, h * hd:(h + 1) * hd] = o.T.astype(o_ref.dtype)


def _moba(pm, vt, batch, seq):
    n = pm.shape[0]
    BLK, hd, H = MOBA_BLOCK, MOBA_HEAD_DIM, MOBA_HEADS
    n_blocks = seq // BLK
    vt_rows = vt.shape[-2]
    return pl.pallas_call(
        _moba_kernel,
        grid=(batch, n_blocks),
        in_specs=[
            pl.BlockSpec((BLK, H * hd), lambda b, i: (b * n_blocks + i, 0)),
            pl.BlockSpec((seq, H * hd), lambda b, i: (b, 1)),
            pl.BlockSpec((None, H, n_blocks, vt_rows, BLK), lambda b, i: (b, 0, 0, 0, 0)),
        ],
        out_specs=pl.BlockSpec((BLK, H * hd), lambda b, i: (b * n_blocks + i, 0)),
        out_shape=jax.ShapeDtypeStruct((n, MOBA_WIDTH), BF16),
        scratch_shapes=[
            pltpu.VMEM((n_blocks, H * hd), F32),
            pltpu.VMEM((H, n_blocks, BLK), F32),
            pltpu.VMEM((H, vt_rows, BLK), F32),
            pltpu.VMEM((5, H, BLK, BLK), F32),
        ],
        compiler_params=pltpu.CompilerParams(
            dimension_semantics=("arbitrary", "arbitrary"),
            vmem_limit_bytes=VMEM_LIMIT_BYTES),
        name="moba",
    )(pm, pm, vt)


def _memkv_kernel(mem_ref, g_ref, wk_ref, wv_ref, k_ref, v_ref):
    h = _rms(mem_ref[...], g_ref[...]).astype(BF16)
    k_ref[...] = _dot(h, wk_ref[...]).astype(BF16)
    v_ref[...] = _dot(h, wv_ref[...]).astype(BF16)


def _memkv(memf, g, wk, wv):
    n = memf.shape[0]
    tm = MEM_LEN
    return pl.pallas_call(
        _memkv_kernel,
        grid=(n // tm,),
        in_specs=[
            pl.BlockSpec((tm, D_MODEL), lambda i: (i, 0)),
            _const_spec((1, D_MODEL)),
            _const_spec((D_MODEL, D_MODEL)),
            _const_spec((D_MODEL, D_MODEL)),
        ],
        out_specs=[pl.BlockSpec((tm, D_MODEL), lambda i: (i, 0))] * 2,
        out_shape=[jax.ShapeDtypeStruct((n, D_MODEL), BF16)] * 2,
        compiler_params=pltpu.CompilerParams(
            dimension_semantics=("arbitrary",), vmem_limit_bytes=VMEM_LIMIT_BYTES),
        name="memkv",
    )(memf, g, wk, wv)


def _mix_xattn_kernel(x_ref, oh_ref, om_ref, wo_ref, g_ref, wq_ref, k_ref, v_ref, wxo_ref, y_ref):
    x1 = (x_ref[...] + _dot(oh_ref[...], wo_ref[:HGRN_WIDTH, :])
          + _dot(om_ref[...], wo_ref[HGRN_WIDTH:, :]))
    h = _rms(x1, g_ref[...]).astype(BF16)
    q = (_dot(h, wq_ref[...]) * (XATTN_HEAD_DIM ** -0.5 * LOG2E)).astype(BF16)
    outs = []
    for hd in range(XATTN_HEADS):
        hs = slice(hd * XATTN_HEAD_DIM, (hd + 1) * XATTN_HEAD_DIM)
        s = _dot_nt(q[:, hs], k_ref[:, hs])
        p = jnp.exp2(s - jnp.max(s, axis=-1, keepdims=True))
        l = jnp.sum(p, axis=-1, keepdims=True)
        outs.append((_dot(p.astype(BF16), v_ref[:, hs]) / l).astype(BF16))
    o = jnp.concatenate(outs, axis=-1)
    y_ref[...] = x1 + _dot(o, wxo_ref[...])


def _mix_xattn(xf, oh, om, wo, g, wq, kmem, vmem, wxo, batch, seq):
    n = xf.shape[0]
    tm = TOKEN_TILE
    steps = seq // tm
    tok = lambda w: pl.BlockSpec((tm, w), lambda b, i: (b * steps + i, 0))
    return pl.pallas_call(
        _mix_xattn_kernel,
        grid=(batch, steps),
        in_specs=[
            tok(D_MODEL), tok(HGRN_WIDTH), tok(MOBA_WIDTH),
            _const_spec((D_MODEL, D_MODEL)),
            _const_spec((1, D_MODEL)),
            _const_spec((D_MODEL, D_MODEL)),
            pl.BlockSpec((MEM_LEN, D_MODEL), lambda b, i: (b, 0)),
            pl.BlockSpec((MEM_LEN, D_MODEL), lambda b, i: (b, 0)),
            _const_spec((D_MODEL, D_MODEL)),
        ],
        out_specs=tok(D_MODEL),
        out_shape=jax.ShapeDtypeStruct((n, D_MODEL), F32),
        compiler_params=pltpu.CompilerParams(
            dimension_semantics=("arbitrary", "arbitrary"), vmem_limit_bytes=VMEM_LIMIT_BYTES),
        name="mix_xattn",
    )(xf, oh, om, wo, g, wq, kmem, vmem, wxo)


def _mlp_kernel(x_ref, g_ref, w1_ref, w2_ref, gf_ref, y_ref, *, final_norm):
    x = x_ref[...]
    h = _rms(x, g_ref[...]).astype(BF16)
    ff_chunk = D_MODEL
    acc = x
    for c in range(D_FF // ff_chunk):
        cs = slice(c * ff_chunk, (c + 1) * ff_chunk)
        z = jnp.maximum(_dot(h, w1_ref[:, cs]), 0.0)
        acc = acc + _dot((z * z).astype(BF16), w2_ref[cs, :])
    y_ref[...] = _rms(acc, gf_ref[...]) if final_norm else acc


def _mlp(xf, g, w1, w2, gf, final_norm):
    n = xf.shape[0]
    tm = TOKEN_TILE
    return pl.pallas_call(
        functools.partial(_mlp_kernel, final_norm=final_norm),
        grid=(n // tm,),
        in_specs=[
            pl.BlockSpec((tm, D_MODEL), lambda i: (i, 0)),
            _const_spec((1, D_MODEL)),
            _const_spec((D_MODEL, D_FF)),
            _const_spec((D_FF, D_MODEL)),
            _const_spec((1, D_MODEL)),
        ],
        out_specs=pl.BlockSpec((tm, D_MODEL), lambda i: (i, 0)),
        out_shape=jax.ShapeDtypeStruct((n, D_MODEL), F32),
        compiler_params=pltpu.CompilerParams(
            dimension_semantics=("arbitrary",), vmem_limit_bytes=VMEM_LIMIT_BYTES),
        name="mlp",
    )(xf, g, w1, w2, gf)


def _rotary_tables(seq):
    half = ROT_DIM // 2
    inv_freq = np.float32(ROPE_THETA) ** (-np.arange(half, dtype=np.float32) * np.float32(2.0 / ROT_DIM))
    ang = (np.arange(seq, dtype=np.float32)[:, None] * inv_freq[None, :]).astype(np.float64)
    cos, sin = np.cos(ang), np.sin(ang)
    rest = MOBA_HEAD_DIM - ROT_DIM
    cos_t = np.concatenate([cos, cos, np.ones((seq, rest))], axis=-1)
    sa_t = np.concatenate([-sin, np.zeros((seq, MOBA_HEAD_DIM - half))], axis=-1)
    sb_t = np.concatenate([np.zeros((seq, half)), sin, np.zeros((seq, rest))], axis=-1)
    return tuple(jnp.asarray(t, F32) for t in (cos_t, sa_t, sb_t))


def _cumsum_matrix():
    T, C = HGRN_TILE, HGRN_CHUNK
    r = np.arange(T)
    same = (r[:, None] // C) == (r[None, :] // C)
    return jnp.asarray(same & (r[None, :] <= r[:, None]), BF16)


def kernel(x, mem, norm_mix, w_in, lb_logits, hgrn_norm, w_out, norm_xattn, norm_mem,
           w_xq, w_xk, w_xv, w_xo, norm_mlp, w_ff1, w_ff2, norm_final):
    batch, seq, _ = x.shape
    n = batch * seq
    xf = x.reshape(n, D_MODEL)
    lb_table = jnp.cumsum(jax.nn.softmax(lb_logits.astype(F32), axis=0), axis=0)
    cos_t, sa_t, sb_t = _rotary_tables(seq)
    cum = _cumsum_matrix()
    row = lambda v: v.reshape(1, -1)
    depth = norm_mix.shape[0]
    for l in range(depth):
        ph, pm, vt = _inproj(xf, row(norm_mix[l]), w_in[l].astype(BF16), cos_t, sa_t, sb_t,
                             batch, seq)
        o_hgrn = _hgrn(ph, row(lb_table[l]), row(hgrn_norm[l]), cum, batch, seq)
        o_moba = _moba(pm, vt, batch, seq)
        kmem, vmem = _memkv(mem.reshape(-1, D_MODEL), row(norm_mem[l]),
                            w_xk[l].astype(BF16), w_xv[l].astype(BF16))
        x2 = _mix_xattn(xf, o_hgrn, o_moba, w_out[l].astype(BF16), row(norm_xattn[l]),
                        w_xq[l].astype(BF16), kmem, vmem, w_xo[l].astype(BF16), batch, seq)
        xf = _mlp(x2, row(norm_mlp[l]), w_ff1[l].astype(BF16), w_ff2[l].astype(BF16),
                  row(norm_final), final_norm=(l == depth - 1))
    return xf.reshape(batch, seq, D_MODEL)
```

```python
import functools
import math

import jax
import jax.numpy as jnp
import numpy as np
from jax import lax
from jax.experimental import pallas as pl
from jax.experimental.pallas import tpu as pltpu

F32 = jnp.float32
BF16 = jnp.bfloat16

D_MODEL = 1024
MEM_LEN = 256
HGRN_HEADS = 4
HGRN_WIDTH = 512
HGRN_HEAD_DIM = 128
HGRN_CHUNK = 32
MOBA_HEADS = 4
MOBA_WIDTH = 512
MOBA_HEAD_DIM = 128
MOBA_BLOCK = 256
MOBA_TOP_K = 3
ROPE_THETA = 500000.0
ROT_DIM = 32
XATTN_HEADS = 4
XATTN_HEAD_DIM = 256
D_FF = 4096
NORM_EPS = 1e-6
IN_COLS = 4 * HGRN_WIDTH + 3 * MOBA_WIDTH

LOG2E = math.log2(math.e)
NEG_BIG = -1e30

VMEM_LIMIT_BYTES = 56 * 1024 * 1024

BF16_SUBLANES = 16
HGRN_TILE = 256
HGRN_SUBTILES = 4
TOKEN_TILE = 1024


def _rms(x, g):
    return x * lax.rsqrt(jnp.mean(x * x, axis=-1, keepdims=True) + NORM_EPS) * g


def _dot(a, b):
    return jnp.dot(a, b, preferred_element_type=F32)


def _dot_nt(a, b):
    return lax.dot_general(a, b, (((1,), (1,)), ((), ())), preferred_element_type=F32)


def _const_spec(shape):
    zeros = (0,) * len(shape)
    return pl.BlockSpec(shape, lambda *_: zeros, pipeline_mode=pl.Buffered(1))


def _inproj_kernel(x_ref, g_ref, w_ref, cos_ref, sa_ref, sb_ref, ph_ref, km_ref, qt_ref, vt_ref):
    h = _rms(x_ref[...], g_ref[...]).astype(BF16)
    hg_cols = 4 * HGRN_WIDTH
    hd, BLK = MOBA_HEAD_DIM, MOBA_BLOCK
    blocks = x_ref.shape[0] // BLK
    ph_ref[...] = _dot(h, w_ref[:, :hg_cols])
    qk = _dot(h, w_ref[:, hg_cols:hg_cols + 2 * MOBA_WIDTH])
    cos, sa, sb = cos_ref[...], sa_ref[...], sb_ref[...]
    q_scale = hd ** -0.5 * LOG2E
    for o in range(0, 2 * MOBA_WIDTH, hd):
        t = qk[:, o:o + hd]
        r = t * cos + pltpu.roll(t, hd - ROT_DIM // 2, 1) * sa + pltpu.roll(t, ROT_DIM // 2, 1) * sb
        if o < MOBA_WIDTH:
            r = r * q_scale
            for blk in range(blocks):
                qt_ref[o // hd, blk] = r[blk * BLK:(blk + 1) * BLK, :].T.astype(BF16)
        else:
            km_ref[:, o - MOBA_WIDTH:o - MOBA_WIDTH + hd] = r.astype(BF16)
    v = _dot(h, w_ref[:, hg_cols + 2 * MOBA_WIDTH:])
    ones_rows = jnp.ones((vt_ref.shape[2] - hd, BLK), BF16)
    for head in range(MOBA_HEADS):
        for blk in range(blocks):
            vt_ref[head, blk, :hd, :] = v[blk * BLK:(blk + 1) * BLK, head * hd:(head + 1) * hd].T.astype(BF16)
            vt_ref[head, blk, hd:, :] = ones_rows


def _inproj(xf, g, w, cos_t, sa_t, sb_t, batch, seq):
    n = xf.shape[0]
    tm = TOKEN_TILE
    tiles_per_seq = seq // tm
    blocks_per_tile = tm // MOBA_BLOCK
    vt_rows = MOBA_HEAD_DIM + BF16_SUBLANES
    rot = lambda: pl.BlockSpec((tm, MOBA_HEAD_DIM), lambda i: (i % tiles_per_seq, 0))
    per_block = lambda rows: pl.BlockSpec(
        (None, MOBA_HEADS, blocks_per_tile, rows, MOBA_BLOCK),
        lambda i: (i // tiles_per_seq, 0, i % tiles_per_seq, 0, 0))
    return pl.pallas_call(
        _inproj_kernel,
        grid=(n // tm,),
        in_specs=[
            pl.BlockSpec((tm, D_MODEL), lambda i: (i, 0)),
            _const_spec((1, D_MODEL)),
            _const_spec((D_MODEL, IN_COLS)),
            rot(), rot(), rot(),
        ],
        out_specs=[
            pl.BlockSpec((tm, 4 * HGRN_WIDTH), lambda i: (i, 0)),
            pl.BlockSpec((tm, MOBA_WIDTH), lambda i: (i, 0)),
            per_block(MOBA_HEAD_DIM),
            per_block(vt_rows),
        ],
        out_shape=[
            jax.ShapeDtypeStruct((n, 4 * HGRN_WIDTH), F32),
            jax.ShapeDtypeStruct((n, MOBA_WIDTH), BF16),
            jax.ShapeDtypeStruct((batch, MOBA_HEADS, seq // MOBA_BLOCK, MOBA_HEAD_DIM, MOBA_BLOCK), BF16),
            jax.ShapeDtypeStruct((batch, MOBA_HEADS, seq // MOBA_BLOCK, vt_rows, MOBA_BLOCK), BF16),
        ],
        compiler_params=pltpu.CompilerParams(
            dimension_semantics=("arbitrary",), vmem_limit_bytes=VMEM_LIMIT_BYTES),
        name="inproj",
    )(xf, g, w, cos_t, sa_t, sb_t)


def _hgrn_kernel(ph_ref, lb_ref, ng_ref, cum_ref, o_ref,
                 st_ref, b_ref, kk_ref, qt_ref, kt_ref, keb_ref, dec_ref, oi_ref):
    T, W, C, dh = HGRN_TILE, HGRN_WIDTH, HGRN_CHUNK, HGRN_HEAD_DIM
    n_chunks = T // C

    @pl.when((pl.program_id(0) == 0) & (pl.program_id(1) == 0))
    def _():
        keb_ref[...] = jnp.zeros_like(keb_ref)

    @pl.when(pl.program_id(1) == 0)
    def _():
        st_ref[...] = jnp.zeros_like(st_ref)

    lb = lb_ref[...]
    f_mid, f_amp = 0.5 * (1.0 + lb), 0.5 * (1.0 - lb)
    row = lax.broadcasted_iota(jnp.int32, (T, T), 0)
    col = lax.broadcasted_iota(jnp.int32, (T, T), 1)
    intra_mask = (row // C == col // C) & (col <= row)
    cum = cum_ref[...]

    for u in range(HGRN_SUBTILES):
        rows = slice(u * T, (u + 1) * T)
        f = f_mid + f_amp * jnp.tanh(0.5 * ph_ref[rows, W:2 * W])
        logf = jnp.log2(f)
        kk_ref[u] = 1.0 - f
        l1 = logf.astype(BF16)
        l2 = (logf - l1.astype(F32)).astype(BF16)
        b_ref[u] = _dot(cum, l1) + _dot(cum, l2)

    for u in range(HGRN_SUBTILES):
        rows = slice(u * T, (u + 1) * T)
        for h in range(HGRN_HEADS):
            hs = slice(h * dh, (h + 1) * dh)
            b = b_ref[u, :, hs]
            qt_ref[u, :, hs] = (ph_ref[rows, h * dh:(h + 1) * dh] * jnp.exp2(b)).astype(BF16)
            k_t = kk_ref[u, :, hs] * jnp.exp2(-b)
            kt_ref[u, :, hs] = k_t.astype(BF16)
            kk_ref[u, :, hs] = k_t
            for c in range(n_chunks):
                cs = slice(c * C, (c + 1) * C)
                decay = jnp.exp2(b_ref[u, (c + 1) * C - 1:(c + 1) * C, hs])
                dec_ref[u, c:c + 1, hs] = decay
                keb_ref[u, h, cs, c * dh:(c + 1) * dh] = (kk_ref[u, cs, hs] * decay).astype(BF16)
            vf = ph_ref[rows, 2 * W + h * dh:2 * W + (h + 1) * dh]
            a = jnp.where(intra_mask, _dot_nt(qt_ref[u, :, hs], kt_ref[u, :, hs]), 0.0)
            upd = _dot(vf.T.astype(BF16), keb_ref[u, h])
            oi_ref[u, h] = _dot(a.astype(BF16), vf.astype(BF16))
            st = st_ref[h]
            for c in range(n_chunks):
                cs = slice(c * C, (c + 1) * C)
                oi_ref[u, h, cs, :] += _dot_nt(qt_ref[u, cs, hs], st.astype(BF16))
                st = st * dec_ref[u, c:c + 1, hs] + upd[:, c * dh:(c + 1) * dh]
            st_ref[h] = st
            o_h = oi_ref[u, h]
            o_h = o_h * lax.rsqrt(jnp.mean(o_h * o_h, axis=-1, keepdims=True) + NORM_EPS)
            g = ph_ref[rows, 3 * W + h * dh:3 * W + (h + 1) * dh]
            o_h = o_h * ng_ref[:, hs] * (0.5 * g * (1.0 + jnp.tanh(0.5 * g)))
            o_ref[rows, hs] = o_h.astype(o_ref.dtype)


def _hgrn(ph, lb, ng, cum, batch, seq):
    n = ph.shape[0]
    T, U = HGRN_TILE, HGRN_SUBTILES
    steps = seq // (T * U)
    n_chunks = T // HGRN_CHUNK
    return pl.pallas_call(
        _hgrn_kernel,
        grid=(batch, steps),
        in_specs=[
            pl.BlockSpec((T * U, 4 * HGRN_WIDTH), lambda b, i: (b * steps + i, 0)),
            _const_spec((1, HGRN_WIDTH)),
            _const_spec((1, HGRN_WIDTH)),
            _const_spec((T, T)),
        ],
        out_specs=pl.BlockSpec((T * U, HGRN_WIDTH), lambda b, i: (b * steps + i, 0)),
        out_shape=jax.ShapeDtypeStruct((n, HGRN_WIDTH), BF16),
        scratch_shapes=[
            pltpu.VMEM((HGRN_HEADS, HGRN_HEAD_DIM, HGRN_HEAD_DIM), F32),
            pltpu.VMEM((U, T, HGRN_WIDTH), F32),
            pltpu.VMEM((U, T, HGRN_WIDTH), F32),
            pltpu.VMEM((U, T, HGRN_WIDTH), BF16),
            pltpu.VMEM((U, T, HGRN_WIDTH), BF16),
            pltpu.VMEM((U, HGRN_HEADS, T, n_chunks * HGRN_HEAD_DIM), BF16),
            pltpu.VMEM((U, n_chunks, HGRN_WIDTH), F32),
            pltpu.VMEM((U, HGRN_HEADS, T, HGRN_HEAD_DIM), F32),
        ],
        compiler_params=pltpu.CompilerParams(
            dimension_semantics=("arbitrary", "arbitrary"), vmem_limit_bytes=VMEM_LIMIT_BYTES),
        name="hgrn",
    )(ph, lb, ng, cum)


def _moba_kernel(qt_ref, k_ref, vt_ref, o_ref, kmean_ref, sel_ref, acc_ref, s_ref):
    BLK, hd, H = MOBA_BLOCK, MOBA_HEAD_DIM, MOBA_HEADS
    n_blocks = k_ref.shape[0] // BLK
    i = pl.program_id(1)

    @pl.when(i == 0)
    def _():
        for j in range(n_blocks):
            kb = k_ref[j * BLK:(j + 1) * BLK, :].astype(F32)
            kmean_ref[j:j + 1, :] = jnp.sum(kb, axis=0, keepdims=True) * (1.0 / BLK)

    blk_id = lax.broadcasted_iota(jnp.int32, (n_blocks, BLK), 0)
    past = blk_id < i
    k_pos = lax.broadcasted_iota(jnp.int32, (BLK, BLK), 0)
    q_pos = lax.broadcasted_iota(jnp.int32, (BLK, BLK), 1)
    causal = k_pos <= q_pos

    def scores_into(slot, j, h):
        rows = pl.ds(pl.multiple_of(jnp.minimum(j, n_blocks - 1) * BLK, BLK), BLK)
        hs = slice(h * hd, (h + 1) * hd)
        s_ref[slot, h] = _dot(k_ref[rows, hs], qt_ref[h])

    SLOT_OWN = 4
    PAIR_A, PAIR_B = (0, 1), (2, 3)
    gates = [_dot(kmean_ref[:, h * hd:(h + 1) * hd].astype(BF16), qt_ref[h])
             for h in range(H)]
    for h in range(H):
        scores_into(SLOT_OWN, i, h)
    for h in range(H):
        scores_into(PAIR_A[0], 0, h)
        scores_into(PAIR_A[1], 1, h)

    m_init = []
    for h in range(H):
        g = jnp.where(past, gates[h], -jnp.inf)
        chosen = jnp.zeros(g.shape, F32)
        for _ in range(MOBA_TOP_K):
            m = jnp.max(g, axis=0, keepdims=True)
            idx = jnp.min(jnp.where(g == m, blk_id, n_blocks), axis=0, keepdims=True)
            hit = blk_id == idx
            chosen = jnp.where(hit, 1.0, chosen)
            g = jnp.where(hit, -jnp.inf, g)
        sel_ref[h] = jnp.where(past, chosen, 0.0)
        s = jnp.where(causal, s_ref[SLOT_OWN, h], NEG_BIG)
        m0 = jnp.max(s, axis=0, keepdims=True)
        acc_ref[h] = _dot(vt_ref[h, i], jnp.exp2(s - m0).astype(BF16))
        m_init.append(m0)

    def consume_pair(slots, j, h, m):
        sx, sy = s_ref[slots[0], h], s_ref[slots[1], h]
        px = sel_ref[h, pl.ds(j, 1), :] > 0.0
        py = sel_ref[h, pl.ds(j + 1, 1), :] > 0.0
        m_new = jnp.maximum(m, jnp.maximum(
            jnp.where(px, jnp.max(sx, axis=0, keepdims=True), NEG_BIG),
            jnp.where(py, jnp.max(sy, axis=0, keepdims=True), NEG_BIG)))
        ex = jnp.exp2(sx - jnp.where(px, m_new, -NEG_BIG)).astype(BF16)
        ey = jnp.exp2(sy - jnp.where(py, m_new, -NEG_BIG)).astype(BF16)
        acc_ref[h] = (jnp.exp2(m - m_new) * acc_ref[h]
                      + (_dot(vt_ref[h, j], ex) + _dot(vt_ref[h, j + 1], ey)))
        return m_new

    def body(t, ms):
        ms = list(ms)
        base = 4 * t

        def produce(pair, h):
            j = base + 2 if pair == "B" else base + 4
            slots = PAIR_B if pair == "B" else PAIR_A
            scores_into(slots[0], j, h)
            scores_into(slots[1], j + 1, h)

        def consume(pair, h):
            j = base if pair == "A" else base + 2
            ms[h] = consume_pair(PAIR_A if pair == "A" else PAIR_B, j, h, ms[h])

        for op in "PB0 PB1 CA0 PB2 CA1 PB3 CA2 PA0 CA3 PA1 CB0 PA2 CB1 PA3 CB2 CB3".split():
            (produce if op[0] == "P" else consume)(op[1], int(op[2]))
        return tuple(ms)

    n_pairs = (i + 1) // 2
    ms = lax.fori_loop(0, n_pairs // 2, body, tuple(m_init))

    @pl.when(n_pairs % 2 == 1)
    def _():
        for h in range(H):
            consume_pair(PAIR_A, 2 * (n_pairs - 1), h, ms[h])

    for h in range(H):
        o = acc_ref[h, :hd, :] / acc_ref[h, hd:hd + 1, :]
        o_ref[:, h * hd:(h + 1) * hd] = o.T.astype(o_ref.dtype)


def _moba(qt, km, vt, batch, seq):
    n = km.shape[0]
    BLK, hd, H = MOBA_BLOCK, MOBA_HEAD_DIM, MOBA_HEADS
    n_blocks = seq // BLK
    vt_rows = vt.shape[-2]
    return pl.pallas_call(
        _moba_kernel,
        grid=(batch, n_blocks),
        in_specs=[
            pl.BlockSpec((None, H, None, hd, BLK), lambda b, i: (b, 0, i, 0, 0)),
            pl.BlockSpec((seq, H * hd), lambda b, i: (b, 0)),
            pl.BlockSpec((None, H, n_blocks, vt_rows, BLK), lambda b, i: (b, 0, 0, 0, 0)),
        ],
        out_specs=pl.BlockSpec((BLK, H * hd), lambda b, i: (b * n_blocks + i, 0)),
        out_shape=jax.ShapeDtypeStruct((n, MOBA_WIDTH), BF16),
        scratch_shapes=[
            pltpu.VMEM((n_blocks, H * hd), F32),
            pltpu.VMEM((H, n_blocks, BLK), F32),
            pltpu.VMEM((H, vt_rows, BLK), F32),
            pltpu.VMEM((5, H, BLK, BLK), F32),
        ],
        compiler_params=pltpu.CompilerParams(
            dimension_semantics=("arbitrary", "arbitrary"),
            vmem_limit_bytes=VMEM_LIMIT_BYTES),
        name="moba",
    )(qt, km, vt)


def _memkv_kernel(mem_ref, g_ref, wk_ref, wv_ref, k_ref, v_ref):
    h = _rms(mem_ref[...], g_ref[...]).astype(BF16)
    k_ref[...] = _dot(h, wk_ref[...]).astype(BF16)
    v_ref[...] = _dot(h, wv_ref[...]).astype(BF16)


def _memkv(memf, g, wk, wv):
    n = memf.shape[0]
    tm = MEM_LEN
    return pl.pallas_call(
        _memkv_kernel,
        grid=(n // tm,),
        in_specs=[
            pl.BlockSpec((tm, D_MODEL), lambda i: (i, 0)),
            _const_spec((1, D_MODEL)),
            _const_spec((D_MODEL, D_MODEL)),
            _const_spec((D_MODEL, D_MODEL)),
        ],
        out_specs=[pl.BlockSpec((tm, D_MODEL), lambda i: (i, 0))] * 2,
        out_shape=[jax.ShapeDtypeStruct((n, D_MODEL), BF16)] * 2,
        compiler_params=pltpu.CompilerParams(
            dimension_semantics=("arbitrary",), vmem_limit_bytes=VMEM_LIMIT_BYTES),
        name="memkv",
    )(memf, g, wk, wv)


def _mix_xattn_kernel(x_ref, oh_ref, om_ref, wo_ref, g_ref, wq_ref, k_ref, v_ref, wxo_ref, y_ref):
    x1 = (x_ref[...] + _dot(oh_ref[...], wo_ref[:HGRN_WIDTH, :])
          + _dot(om_ref[...], wo_ref[HGRN_WIDTH:, :]))
    h = _rms(x1, g_ref[...]).astype(BF16)
    q = (_dot(h, wq_ref[...]) * (XATTN_HEAD_DIM ** -0.5 * LOG2E)).astype(BF16)
    outs = []
    for hd in range(XATTN_HEADS):
        hs = slice(hd * XATTN_HEAD_DIM, (hd + 1) * XATTN_HEAD_DIM)
        s = _dot_nt(q[:, hs], k_ref[:, hs])
        p = jnp.exp2(s - jnp.max(s, axis=-1, keepdims=True))
        l = jnp.sum(p, axis=-1, keepdims=True)
        outs.append((_dot(p.astype(BF16), v_ref[:, hs]) / l).astype(BF16))
    o = jnp.concatenate(outs, axis=-1)
    y_ref[...] = x1 + _dot(o, wxo_ref[...])


def _mix_xattn(xf, oh, om, wo, g, wq, kmem, vmem, wxo, batch, seq):
    n = xf.shape[0]
    tm = TOKEN_TILE
    steps = seq // tm
    tok = lambda w: pl.BlockSpec((tm, w), lambda b, i: (b * steps + i, 0))
    return pl.pallas_call(
        _mix_xattn_kernel,
        grid=(batch, steps),
        in_specs=[
            tok(D_MODEL), tok(HGRN_WIDTH), tok(MOBA_WIDTH),
            _const_spec((D_MODEL, D_MODEL)),
            _const_spec((1, D_MODEL)),
            _const_spec((D_MODEL, D_MODEL)),
            pl.BlockSpec((MEM_LEN, D_MODEL), lambda b, i: (b, 0)),
            pl.BlockSpec((MEM_LEN, D_MODEL), lambda b, i: (b, 0)),
            _const_spec((D_MODEL, D_MODEL)),
        ],
        out_specs=tok(D_MODEL),
        out_shape=jax.ShapeDtypeStruct((n, D_MODEL), F32),
        compiler_params=pltpu.CompilerParams(
            dimension_semantics=("arbitrary", "arbitrary"), vmem_limit_bytes=VMEM_LIMIT_BYTES),
        name="mix_xattn",
    )(xf, oh, om, wo, g, wq, kmem, vmem, wxo)


def _mlp_kernel(x_ref, g_ref, w1_ref, w2_ref, gf_ref, y_ref, *, final_norm):
    x = x_ref[...]
    h = _rms(x, g_ref[...]).astype(BF16)
    ff_chunk = D_MODEL
    acc = x
    for c in range(D_FF // ff_chunk):
        cs = slice(c * ff_chunk, (c + 1) * ff_chunk)
        z = jnp.maximum(_dot(h, w1_ref[:, cs]), 0.0)
        acc = acc + _dot((z * z).astype(BF16), w2_ref[cs, :])
    y_ref[...] = _rms(acc, gf_ref[...]) if final_norm else acc


def _mlp(xf, g, w1, w2, gf, final_norm):
    n = xf.shape[0]
    tm = TOKEN_TILE
    return pl.pallas_call(
        functools.partial(_mlp_kernel, final_norm=final_norm),
        grid=(n // tm,),
        in_specs=[
            pl.BlockSpec((tm, D_MODEL), lambda i: (i, 0)),
            _const_spec((1, D_MODEL)),
            _const_spec((D_MODEL, D_FF)),
            _const_spec((D_FF, D_MODEL)),
            _const_spec((1, D_MODEL)),
        ],
        out_specs=pl.BlockSpec((tm, D_MODEL), lambda i: (i, 0)),
        out_shape=jax.ShapeDtypeStruct((n, D_MODEL), F32),
        compiler_params=pltpu.CompilerParams(
            dimension_semantics=("arbitrary",), vmem_limit_bytes=VMEM_LIMIT_BYTES),
        name="mlp",
    )(xf, g, w1, w2, gf)


def _rotary_tables(seq):
    half = ROT_DIM // 2
    inv_freq = np.float32(ROPE_THETA) ** (-np.arange(half, dtype=np.float32) * np.float32(2.0 / ROT_DIM))
    ang = (np.arange(seq, dtype=np.float32)[:, None] * inv_freq[None, :]).astype(np.float64)
    cos, sin = np.cos(ang), np.sin(ang)
    rest = MOBA_HEAD_DIM - ROT_DIM
    cos_t = np.concatenate([cos, cos, np.ones((seq, rest))], axis=-1)
    sa_t = np.concatenate([-sin, np.zeros((seq, MOBA_HEAD_DIM - half))], axis=-1)
    sb_t = np.concatenate([np.zeros((seq, half)), sin, np.zeros((seq, rest))], axis=-1)
    return tuple(jnp.asarray(t, F32) for t in (cos_t, sa_t, sb_t))


def _cumsum_matrix():
    T, C = HGRN_TILE, HGRN_CHUNK
    r = np.arange(T)
    same = (r[:, None] // C) == (r[None, :] // C)
    return jnp.asarray(same & (r[None, :] <= r[:, None]), BF16)


def kernel(x, mem, norm_mix, w_in, lb_logits, hgrn_norm, w_out, norm_xattn, norm_mem,
           w_xq, w_xk, w_xv, w_xo, norm_mlp, w_ff1, w_ff2, norm_final):
    batch, seq, _ = x.shape
    n = batch * seq
    xf = x.reshape(n, D_MODEL)
    lb_table = jnp.cumsum(jax.nn.softmax(lb_logits.astype(F32), axis=0), axis=0)
    cos_t, sa_t, sb_t = _rotary_tables(seq)
    cum = _cumsum_matrix()
    row = lambda v: v.reshape(1, -1)
    depth = norm_mix.shape[0]
    for l in range(depth):
        ph, km, qt, vt = _inproj(xf, row(norm_mix[l]), w_in[l].astype(BF16), cos_t, sa_t, sb_t,
                             batch, seq)
        o_hgrn = _hgrn(ph, row(lb_table[l]), row(hgrn_norm[l]), cum, batch, seq)
        o_moba = _moba(qt, km, vt, batch, seq)
        kmem, vmem = _memkv(mem.reshape(-1, D_MODEL), row(norm_mem[l]),
                            w_xk[l].astype(BF16), w_xv[l].astype(BF16))
        x2 = _mix_xattn(xf, o_hgrn, o_moba, w_out[l].astype(BF16), row(norm_xattn[l]),
                        w_xq[l].astype(BF16), kmem, vmem, w_xo[l].astype(BF16), batch, seq)
        xf = _mlp(x2, row(norm_mlp[l]), w_ff1[l].astype(BF16), w_ff2[l].astype(BF16),
                  row(norm_final), final_norm=(l == depth - 1))
    return xf.reshape(batch, seq, D_MODEL)
```

```python
import functools
import math

import jax
import jax.numpy as jnp
import numpy as np
from jax import lax
from jax.experimental import pallas as pl
from jax.experimental.pallas import tpu as pltpu

F32 = jnp.float32
BF16 = jnp.bfloat16

D_MODEL = 1024
MEM_LEN = 256
HGRN_HEADS = 4
HGRN_WIDTH = 512
HGRN_HEAD_DIM = 128
HGRN_CHUNK = 32
MOBA_HEADS = 4
MOBA_WIDTH = 512
MOBA_HEAD_DIM = 128
MOBA_BLOCK = 256
MOBA_TOP_K = 3
ROPE_THETA = 500000.0
ROT_DIM = 32
XATTN_HEADS = 4
XATTN_HEAD_DIM = 256
D_FF = 4096
NORM_EPS = 1e-6
IN_COLS = 4 * HGRN_WIDTH + 3 * MOBA_WIDTH

LOG2E = math.log2(math.e)
NEG_BIG = -1e30

VMEM_LIMIT_BYTES = 56 * 1024 * 1024

BF16_SUBLANES = 16
HGRN_TILE = 256
HGRN_SUBTILES = 4
TOKEN_TILE = 1024


def _rms(x, g):
    return x * lax.rsqrt(jnp.mean(x * x, axis=-1, keepdims=True) + NORM_EPS) * g


def _dot(a, b):
    return jnp.dot(a, b, preferred_element_type=F32)


def _dot_nt(a, b):
    return lax.dot_general(a, b, (((1,), (1,)), ((), ())), preferred_element_type=F32)


def _const_spec(shape):
    zeros = (0,) * len(shape)
    return pl.BlockSpec(shape, lambda *_: zeros, pipeline_mode=pl.Buffered(1))


def _inproj_kernel(x_ref, g_ref, w_ref, cos_ref, sa_ref, sb_ref, ph_ref, pm_ref, vt_ref):
    h = _rms(x_ref[...], g_ref[...]).astype(BF16)
    hg_cols = 4 * HGRN_WIDTH
    hd, BLK = MOBA_HEAD_DIM, MOBA_BLOCK
    ph_ref[...] = _dot(h, w_ref[:, :hg_cols])
    qk = _dot(h, w_ref[:, hg_cols:hg_cols + 2 * MOBA_WIDTH])
    cos, sa, sb = cos_ref[...], sa_ref[...], sb_ref[...]
    q_scale = hd ** -0.5 * LOG2E
    for o in range(0, 2 * MOBA_WIDTH, hd):
        t = qk[:, o:o + hd]
        r = t * cos + pltpu.roll(t, hd - ROT_DIM // 2, 1) * sa + pltpu.roll(t, ROT_DIM // 2, 1) * sb
        if o < MOBA_WIDTH:
            r = r * q_scale
        pm_ref[:, o:o + hd] = r.astype(BF16)
    v = _dot(h, w_ref[:, hg_cols + 2 * MOBA_WIDTH:])
    ones_rows = jnp.ones((vt_ref.shape[2] - hd, BLK), BF16)
    for head in range(MOBA_HEADS):
        for blk in range(v.shape[0] // BLK):
            vt_ref[head, blk, :hd, :] = v[blk * BLK:(blk + 1) * BLK, head * hd:(head + 1) * hd].T.astype(BF16)
            vt_ref[head, blk, hd:, :] = ones_rows


def _inproj(xf, g, w, cos_t, sa_t, sb_t, batch, seq):
    n = xf.shape[0]
    tm = TOKEN_TILE
    tiles_per_seq = seq // tm
    blocks_per_tile = tm // MOBA_BLOCK
    vt_rows = MOBA_HEAD_DIM + BF16_SUBLANES
    rot = lambda: pl.BlockSpec((tm, MOBA_HEAD_DIM), lambda i: (i % tiles_per_seq, 0))
    per_block = lambda rows: pl.BlockSpec(
        (None, MOBA_HEADS, blocks_per_tile, rows, MOBA_BLOCK),
        lambda i: (i // tiles_per_seq, 0, i % tiles_per_seq, 0, 0))
    return pl.pallas_call(
        _inproj_kernel,
        grid=(n // tm,),
        in_specs=[
            pl.BlockSpec((tm, D_MODEL), lambda i: (i, 0)),
            _const_spec((1, D_MODEL)),
            _const_spec((D_MODEL, IN_COLS)),
            rot(), rot(), rot(),
        ],
        out_specs=[
            pl.BlockSpec((tm, 4 * HGRN_WIDTH), lambda i: (i, 0)),
            pl.BlockSpec((tm, 2 * MOBA_WIDTH), lambda i: (i, 0)),
            per_block(vt_rows),
        ],
        out_shape=[
            jax.ShapeDtypeStruct((n, 4 * HGRN_WIDTH), F32),
            jax.ShapeDtypeStruct((n, 2 * MOBA_WIDTH), BF16),
            jax.ShapeDtypeStruct((batch, MOBA_HEADS, seq // MOBA_BLOCK, vt_rows, MOBA_BLOCK), BF16),
        ],
        compiler_params=pltpu.CompilerParams(
            dimension_semantics=("arbitrary",), vmem_limit_bytes=VMEM_LIMIT_BYTES),
        name="inproj",
    )(xf, g, w, cos_t, sa_t, sb_t)


def _hgrn_kernel(ph_ref, lb_ref, ng_ref, cum_ref, o_ref,
                 st_ref, b_ref, kk_ref, qt_ref, kt_ref, keb_ref, dec_ref, oi_ref):
    T, W, C, dh = HGRN_TILE, HGRN_WIDTH, HGRN_CHUNK, HGRN_HEAD_DIM
    n_chunks = T // C

    @pl.when((pl.program_id(0) == 0) & (pl.program_id(1) == 0))
    def _():
        keb_ref[...] = jnp.zeros_like(keb_ref)

    @pl.when(pl.program_id(1) == 0)
    def _():
        st_ref[...] = jnp.zeros_like(st_ref)

    lb = lb_ref[...]
    f_mid, f_amp = 0.5 * (1.0 + lb), 0.5 * (1.0 - lb)
    row = lax.broadcasted_iota(jnp.int32, (T, T), 0)
    col = lax.broadcasted_iota(jnp.int32, (T, T), 1)
    intra_mask = (row // C == col // C) & (col <= row)
    cum = cum_ref[...]

    for u in range(HGRN_SUBTILES):
        rows = slice(u * T, (u + 1) * T)
        f = f_mid + f_amp * jnp.tanh(0.5 * ph_ref[rows, W:2 * W])
        logf = jnp.log2(f)
        kk_ref[u] = 1.0 - f
        l1 = logf.astype(BF16)
        l2 = (logf - l1.astype(F32)).astype(BF16)
        b_ref[u] = _dot(cum, l1) + _dot(cum, l2)

    for u in range(HGRN_SUBTILES):
        rows = slice(u * T, (u + 1) * T)
        for h in range(HGRN_HEADS):
            hs = slice(h * dh, (h + 1) * dh)
            b = b_ref[u, :, hs]
            qt_ref[u, :, hs] = (ph_ref[rows, h * dh:(h + 1) * dh] * jnp.exp2(b)).astype(BF16)
            k_t = kk_ref[u, :, hs] * jnp.exp2(-b)
            kt_ref[u, :, hs] = k_t.astype(BF16)
            kk_ref[u, :, hs] = k_t
            for c in range(n_chunks):
                cs = slice(c * C, (c + 1) * C)
                decay = jnp.exp2(b_ref[u, (c + 1) * C - 1:(c + 1) * C, hs])
                dec_ref[u, c:c + 1, hs] = decay
                keb_ref[u, h, cs, c * dh:(c + 1) * dh] = (kk_ref[u, cs, hs] * decay).astype(BF16)
            vf = ph_ref[rows, 2 * W + h * dh:2 * W + (h + 1) * dh]
            a = jnp.where(intra_mask, _dot_nt(qt_ref[u, :, hs], kt_ref[u, :, hs]), 0.0)
            upd = _dot(vf.T.astype(BF16), keb_ref[u, h])
            oi_ref[u, h] = _dot(a.astype(BF16), vf.astype(BF16))
            st = st_ref[h]
            for c in range(n_chunks):
                cs = slice(c * C, (c + 1) * C)
                oi_ref[u, h, cs, :] += _dot_nt(qt_ref[u, cs, hs], st.astype(BF16))
                st = st * dec_ref[u, c:c + 1, hs] + upd[:, c * dh:(c + 1) * dh]
            st_ref[h] = st
            o_h = oi_ref[u, h]
            o_h = o_h * lax.rsqrt(jnp.mean(o_h * o_h, axis=-1, keepdims=True) + NORM_EPS)
            g = ph_ref[rows, 3 * W + h * dh:3 * W + (h + 1) * dh]
            o_h = o_h * ng_ref[:, hs] * (0.5 * g * (1.0 + jnp.tanh(0.5 * g)))
            o_ref[rows, hs] = o_h.astype(o_ref.dtype)


def _hgrn(ph, lb, ng, cum, batch, seq):
    n = ph.shape[0]
    T, U = HGRN_TILE, HGRN_SUBTILES
    steps = seq // (T * U)
    n_chunks = T // HGRN_CHUNK
    return pl.pallas_call(
        _hgrn_kernel,
        grid=(batch, steps),
        in_specs=[
            pl.BlockSpec((T * U, 4 * HGRN_WIDTH), lambda b, i: (b * steps + i, 0)),
            _const_spec((1, HGRN_WIDTH)),
            _const_spec((1, HGRN_WIDTH)),
            _const_spec((T, T)),
        ],
        out_specs=pl.BlockSpec((T * U, HGRN_WIDTH), lambda b, i: (b * steps + i, 0)),
        out_shape=jax.ShapeDtypeStruct((n, HGRN_WIDTH), BF16),
        scratch_shapes=[
            pltpu.VMEM((HGRN_HEADS, HGRN_HEAD_DIM, HGRN_HEAD_DIM), F32),
            pltpu.VMEM((U, T, HGRN_WIDTH), F32),
            pltpu.VMEM((U, T, HGRN_WIDTH), F32),
            pltpu.VMEM((U, T, HGRN_WIDTH), BF16),
            pltpu.VMEM((U, T, HGRN_WIDTH), BF16),
            pltpu.VMEM((U, HGRN_HEADS, T, n_chunks * HGRN_HEAD_DIM), BF16),
            pltpu.VMEM((U, n_chunks, HGRN_WIDTH), F32),
            pltpu.VMEM((U, HGRN_HEADS, T, HGRN_HEAD_DIM), F32),
        ],
        compiler_params=pltpu.CompilerParams(
            dimension_semantics=("arbitrary", "arbitrary"), vmem_limit_bytes=VMEM_LIMIT_BYTES),
        name="hgrn",
    )(ph, lb, ng, cum)


def _moba_kernel(q_ref, k_ref, vt_ref, o_ref, kmean_ref, sel_ref, acc_ref, s_ref):
    BLK, hd, H = MOBA_BLOCK, MOBA_HEAD_DIM, MOBA_HEADS
    n_blocks = k_ref.shape[0] // BLK
    i = pl.program_id(1)

    @pl.when(i == 0)
    def _():
        for j in range(n_blocks):
            kb = k_ref[j * BLK:(j + 1) * BLK, :].astype(F32)
            kmean_ref[j:j + 1, :] = jnp.sum(kb, axis=0, keepdims=True) * (1.0 / BLK)

    blk_id = lax.broadcasted_iota(jnp.int32, (n_blocks, BLK), 0)
    past = blk_id < i
    k_pos = lax.broadcasted_iota(jnp.int32, (BLK, BLK), 0)
    q_pos = lax.broadcasted_iota(jnp.int32, (BLK, BLK), 1)
    causal = k_pos <= q_pos

    def scores_into(slot, j, h):
        rows = pl.ds(pl.multiple_of(jnp.minimum(j, n_blocks - 1) * BLK, BLK), BLK)
        hs = slice(h * hd, (h + 1) * hd)
        s_ref[slot, h] = _dot_nt(k_ref[rows, hs], q_ref[:, hs])

    SLOT_OWN = 4
    PAIR_A, PAIR_B = (0, 1), (2, 3)
    gates = [_dot_nt(kmean_ref[:, h * hd:(h + 1) * hd].astype(BF16), q_ref[:, h * hd:(h + 1) * hd])
             for h in range(H)]
    for h in range(H):
        scores_into(SLOT_OWN, i, h)
    for h in range(H):
        scores_into(PAIR_A[0], 0, h)
        scores_into(PAIR_A[1], 1, h)

    m_init = []
    for h in range(H):
        g = jnp.where(past, gates[h], -jnp.inf)
        chosen = jnp.zeros(g.shape, F32)
        for _ in range(MOBA_TOP_K):
            m = jnp.max(g, axis=0, keepdims=True)
            idx = jnp.min(jnp.where(g == m, blk_id, n_blocks), axis=0, keepdims=True)
            hit = blk_id == idx
            chosen = jnp.where(hit, 1.0, chosen)
            g = jnp.where(hit, -jnp.inf, g)
        sel_ref[h] = jnp.where(past, chosen, 0.0)
        s = jnp.where(causal, s_ref[SLOT_OWN, h], NEG_BIG)
        m0 = jnp.max(s, axis=0, keepdims=True)
        acc_ref[h] = _dot(vt_ref[h, i], jnp.exp2(s - m0).astype(BF16))
        m_init.append(m0)

    def consume_pair(slots, j, h, m):
        sx, sy = s_ref[slots[0], h], s_ref[slots[1], h]
        px = sel_ref[h, pl.ds(j, 1), :] > 0.0
        py = sel_ref[h, pl.ds(j + 1, 1), :] > 0.0
        m_new = jnp.maximum(m, jnp.maximum(
            jnp.where(px, jnp.max(sx, axis=0, keepdims=True), NEG_BIG),
            jnp.where(py, jnp.max(sy, axis=0, keepdims=True), NEG_BIG)))
        ex = jnp.exp2(sx - jnp.where(px, m_new, -NEG_BIG)).astype(BF16)
        ey = jnp.exp2(sy - jnp.where(py, m_new, -NEG_BIG)).astype(BF16)
        acc_ref[h] = (jnp.exp2(m - m_new) * acc_ref[h]
                      + (_dot(vt_ref[h, j], ex) + _dot(vt_ref[h, j + 1], ey)))
        return m_new

    def body(t, ms):
        ms = list(ms)
        base = 4 * t

        def produce(pair, h):
            j = base + 2 if pair == "B" else base + 4
            slots = PAIR_B if pair == "B" else PAIR_A
            scores_into(slots[0], j, h)
            scores_into(slots[1], j + 1, h)

        def consume(pair, h):
            j = base if pair == "A" else base + 2
            ms[h] = consume_pair(PAIR_A if pair == "A" else PAIR_B, j, h, ms[h])

        for op in "PB0 PB1 CA0 PB2 CA1 PB3 CA2 PA0 CA3 PA1 CB0 PA2 CB1 PA3 CB2 CB3".split():
            (produce if op[0] == "P" else consume)(op[1], int(op[2]))
        return tuple(ms)

    n_pairs = (i + 1) // 2
    ms = lax.fori_loop(0, n_pairs // 2, body, tuple(m_init))

    @pl.when(n_pairs % 2 == 1)
    def _():
        for h in range(H):
            consume_pair(PAIR_A, 2 * (n_pairs - 1), h, ms[h])

    for h in range(H):
        o = acc_ref[h, :hd, :] / acc_ref[h, hd:hd + 1, :]
        o_ref[:, h * hd:(h + 1) * hd] = o.T.astype(o_ref.dtype)


def _moba(pm, vt, batch, seq):
    n = pm.shape[0]
    BLK, hd, H = MOBA_BLOCK, MOBA_HEAD_DIM, MOBA_HEADS
    n_blocks = seq // BLK
    vt_rows = vt.shape[-2]
    return pl.pallas_call(
        _moba_kernel,
        grid=(batch, n_blocks),
        in_specs=[
            pl.BlockSpec((BLK, H * hd), lambda b, i: (b * n_blocks + i, 0)),
            pl.BlockSpec((seq, H * hd), lambda b, i: (b, 1)),
            pl.BlockSpec((None, H, n_blocks, vt_rows, BLK), lambda b, i: (b, 0, 0, 0, 0)),
        ],
        out_specs=pl.BlockSpec((BLK, H * hd), lambda b, i: (b * n_blocks + i, 0)),
        out_shape=jax.ShapeDtypeStruct((n, MOBA_WIDTH), BF16),
        scratch_shapes=[
            pltpu.VMEM((n_blocks, H * hd), F32),
            pltpu.VMEM((H, n_blocks, BLK), F32),
            pltpu.VMEM((H, vt_rows, BLK), F32),
            pltpu.VMEM((5, H, BLK, BLK), F32),
        ],
        compiler_params=pltpu.CompilerParams(
            dimension_semantics=("arbitrary", "arbitrary"),
            vmem_limit_bytes=VMEM_LIMIT_BYTES),
        name="moba",
    )(pm, pm, vt)


def _mix_xattn_kernel(x_ref, oh_ref, om_ref, wo_ref, g_ref, wq_ref, mem_ref, gm_ref, wk_ref, wv_ref,
                      wxo_ref, y_ref, k_ref, v_ref):
    @pl.when(pl.program_id(1) == 0)
    def _():
        hm = _rms(mem_ref[...], gm_ref[...]).astype(BF16)
        k_ref[...] = _dot(hm, wk_ref[...]).astype(BF16)
        v_ref[...] = _dot(hm, wv_ref[...]).astype(BF16)

    x1 = (x_ref[...] + _dot(oh_ref[...], wo_ref[:HGRN_WIDTH, :])
          + _dot(om_ref[...], wo_ref[HGRN_WIDTH:, :]))
    h = _rms(x1, g_ref[...]).astype(BF16)
    q = (_dot(h, wq_ref[...]) * (XATTN_HEAD_DIM ** -0.5 * LOG2E)).astype(BF16)
    outs = []
    for hd in range(XATTN_HEADS):
        hs = slice(hd * XATTN_HEAD_DIM, (hd + 1) * XATTN_HEAD_DIM)
        s = _dot_nt(q[:, hs], k_ref[:, hs])
        p = jnp.exp2(s - jnp.max(s, axis=-1, keepdims=True))
        l = jnp.sum(p, axis=-1, keepdims=True)
        outs.append((_dot(p.astype(BF16), v_ref[:, hs]) / l).astype(BF16))
    o = jnp.concatenate(outs, axis=-1)
    y_ref[...] = x1 + _dot(o, wxo_ref[...])


def _mix_xattn(xf, oh, om, wo, g, wq, memf, gm, wk, wv, wxo, batch, seq):
    n = xf.shape[0]
    tm = TOKEN_TILE
    steps = seq // tm
    tok = lambda w: pl.BlockSpec((tm, w), lambda b, i: (b * steps + i, 0))
    return pl.pallas_call(
        _mix_xattn_kernel,
        grid=(batch, steps),
        in_specs=[
            tok(D_MODEL), tok(HGRN_WIDTH), tok(MOBA_WIDTH),
            _const_spec((D_MODEL, D_MODEL)),
            _const_spec((1, D_MODEL)),
            _const_spec((D_MODEL, D_MODEL)),
            pl.BlockSpec((MEM_LEN, D_MODEL), lambda b, i: (b, 0)),
            _const_spec((1, D_MODEL)),
            _const_spec((D_MODEL, D_MODEL)),
            _const_spec((D_MODEL, D_MODEL)),
            _const_spec((D_MODEL, D_MODEL)),
        ],
        out_specs=tok(D_MODEL),
        out_shape=jax.ShapeDtypeStruct((n, D_MODEL), F32),
        scratch_shapes=[pltpu.VMEM((MEM_LEN, D_MODEL), BF16), pltpu.VMEM((MEM_LEN, D_MODEL), BF16)],
        compiler_params=pltpu.CompilerParams(
            dimension_semantics=("arbitrary", "arbitrary"), vmem_limit_bytes=VMEM_LIMIT_BYTES),
        name="mix_xattn",
    )(xf, oh, om, wo, g, wq, memf, gm, wk, wv, wxo)


def _mlp_kernel(x_ref, g_ref, w1_ref, w2_ref, gf_ref, y_ref, *, final_norm):
    x = x_ref[...]
    h = _rms(x, g_ref[...]).astype(BF16)
    ff_chunk = D_MODEL
    acc = x
    for c in range(D_FF // ff_chunk):
        cs = slice(c * ff_chunk, (c + 1) * ff_chunk)
        z = jnp.maximum(_dot(h, w1_ref[:, cs]), 0.0)
        acc = acc + _dot((z * z).astype(BF16), w2_ref[cs, :])
    y_ref[...] = _rms(acc, gf_ref[...]) if final_norm else acc


def _mlp(xf, g, w1, w2, gf, final_norm):
    n = xf.shape[0]
    tm = TOKEN_TILE
    return pl.pallas_call(
        functools.partial(_mlp_kernel, final_norm=final_norm),
        grid=(n // tm,),
        in_specs=[
            pl.BlockSpec((tm, D_MODEL), lambda i: (i, 0)),
            _const_spec((1, D_MODEL)),
            _const_spec((D_MODEL, D_FF)),
            _const_spec((D_FF, D_MODEL)),
            _const_spec((1, D_MODEL)),
        ],
        out_specs=pl.BlockSpec((tm, D_MODEL), lambda i: (i, 0)),
        out_shape=jax.ShapeDtypeStruct((n, D_MODEL), F32),
        compiler_params=pltpu.CompilerParams(
            dimension_semantics=("arbitrary",), vmem_limit_bytes=VMEM_LIMIT_BYTES),
        name="mlp",
    )(xf, g, w1, w2, gf)


def _rotary_tables(seq):
    half = ROT_DIM // 2
    inv_freq = np.float32(ROPE_THETA) ** (-np.arange(half, dtype=np.float32) * np.float32(2.0 / ROT_DIM))
    ang = (np.arange(seq, dtype=np.float32)[:, None] * inv_freq[None, :]).astype(np.float64)
    cos, sin = np.cos(ang), np.sin(ang)
    rest = MOBA_HEAD_DIM - ROT_DIM
    cos_t = np.concatenate([cos, cos, np.ones((seq, rest))], axis=-1)
    sa_t = np.concatenate([-sin, np.zeros((seq, MOBA_HEAD_DIM - half))], axis=-1)
    sb_t = np.concatenate([np.zeros((seq, half)), sin, np.zeros((seq, rest))], axis=-1)
    return tuple(jnp.asarray(t, F32) for t in (cos_t, sa_t, sb_t))


def _cumsum_matrix():
    T, C = HGRN_TILE, HGRN_CHUNK
    r = np.arange(T)
    same = (r[:, None] // C) == (r[None, :] // C)
    return jnp.asarray(same & (r[None, :] <= r[:, None]), BF16)


def kernel(x, mem, norm_mix, w_in, lb_logits, hgrn_norm, w_out, norm_xattn, norm_mem,
           w_xq, w_xk, w_xv, w_xo, norm_mlp, w_ff1, w_ff2, norm_final):
    batch, seq, _ = x.shape
    n = batch * seq
    xf = x.reshape(n, D_MODEL)
    lb_table = jnp.cumsum(jax.nn.softmax(lb_logits.astype(F32), axis=0), axis=0)
    cos_t, sa_t, sb_t = _rotary_tables(seq)
    cum = _cumsum_matrix()
    row = lambda v: v.reshape(1, -1)
    depth = norm_mix.shape[0]
    for l in range(depth):
        ph, pm, vt = _inproj(xf, row(norm_mix[l]), w_in[l].astype(BF16), cos_t, sa_t, sb_t,
                             batch, seq)
        o_hgrn = _hgrn(ph, row(lb_table[l]), row(hgrn_norm[l]), cum, batch, seq)
        o_moba = _moba(pm, vt, batch, seq)
        x2 = _mix_xattn(xf, o_hgrn, o_moba, w_out[l].astype(BF16), row(norm_xattn[l]),
                        w_xq[l].astype(BF16), mem.reshape(-1, D_MODEL), row(norm_mem[l]),
                        w_xk[l].astype(BF16), w_xv[l].astype(BF16), w_xo[l].astype(BF16),
                        batch, seq)
        xf = _mlp(x2, row(norm_mlp[l]), w_ff1[l].astype(BF16), w_ff2[l].astype(BF16),
                  row(norm_final), final_norm=(l == depth - 1))
    return xf.reshape(batch, seq, D_MODEL)
```

```python
import functools
import math

import jax
import jax.numpy as jnp
import numpy as np
from jax import lax
from jax.experimental import pallas as pl
from jax.experimental.pallas import tpu as pltpu

F32 = jnp.float32
BF16 = jnp.bfloat16

D_MODEL = 1024
MEM_LEN = 256
HGRN_HEADS = 4
HGRN_WIDTH = 512
HGRN_HEAD_DIM = 128
HGRN_CHUNK = 32
MOBA_HEADS = 4
MOBA_WIDTH = 512
MOBA_HEAD_DIM = 128
MOBA_BLOCK = 256
MOBA_TOP_K = 3
ROPE_THETA = 500000.0
ROT_DIM = 32
XATTN_HEADS = 4
XATTN_HEAD_DIM = 256
D_FF = 4096
NORM_EPS = 1e-6
IN_COLS = 4 * HGRN_WIDTH + 3 * MOBA_WIDTH

LOG2E = math.log2(math.e)
NEG_BIG = -1e30

VMEM_LIMIT_BYTES = 56 * 1024 * 1024

BF16_SUBLANES = 16
HGRN_TILE = 256
HGRN_SUBTILES = 4
TOKEN_TILE = 1024


def _rms(x, g):
    return x * lax.rsqrt(jnp.mean(x * x, axis=-1, keepdims=True) + NORM_EPS) * g


def _dot(a, b):
    return jnp.dot(a, b, preferred_element_type=F32)


def _dot_nt(a, b):
    return lax.dot_general(a, b, (((1,), (1,)), ((), ())), preferred_element_type=F32)


def _const_spec(shape):
    zeros = (0,) * len(shape)
    return pl.BlockSpec(shape, lambda *_: zeros, pipeline_mode=pl.Buffered(1))


def _inproj_kernel(x_ref, g_ref, w_ref, cos_ref, sa_ref, sb_ref, lb_ref, ng_ref, cum_ref,
                   qt_ref, kt_ref, ke_ref, dec_ref, v_ref, gate_ref, pm_ref, vt_ref, b_ref, k32_ref):
    h = _rms(x_ref[...], g_ref[...]).astype(BF16)
    W, C, T = HGRN_WIDTH, HGRN_CHUNK, HGRN_TILE
    hg_cols = 4 * W
    hd, BLK = MOBA_HEAD_DIM, MOBA_BLOCK
    tm = x_ref.shape[0]

    lb = lb_ref[...]
    f = 0.5 * (1.0 + lb) + (0.5 * (1.0 - lb)) * jnp.tanh(0.5 * _dot(h, w_ref[:, W:2 * W]))
    logf = jnp.log2(f)
    l1 = logf.astype(BF16)
    l2 = (logf - l1.astype(F32)).astype(BF16)
    cum = cum_ref[...]
    for t in range(tm // T):
        rows = slice(t * T, (t + 1) * T)
        b_ref[rows, :] = _dot(cum, l1[rows, :]) + _dot(cum, l2[rows, :])
    b = b_ref[...]
    qt_ref[...] = (_dot(h, w_ref[:, 0:W]) * jnp.exp2(b)).astype(BF16)
    k32 = (1.0 - f) * jnp.exp2(-b)
    kt_ref[...] = k32.astype(BF16)
    k32_ref[...] = k32
    for c in range(tm // C):
        cs = slice(c * C, (c + 1) * C)
        decay = jnp.exp2(b_ref[(c + 1) * C - 1:(c + 1) * C, :])
        dec_ref[c:c + 1, :] = decay
        ke_ref[cs, :] = (k32_ref[cs, :] * decay).astype(BF16)
    v_ref[...] = _dot(h, w_ref[:, 2 * W:3 * W]).astype(BF16)
    hg = _dot(h, w_ref[:, 3 * W:4 * W])
    gate_ref[...] = (ng_ref[...] * (0.5 * hg * (1.0 + jnp.tanh(0.5 * hg)))).astype(BF16)

    qk = _dot(h, w_ref[:, hg_cols:hg_cols + 2 * MOBA_WIDTH])
    cos, sa, sb = cos_ref[...], sa_ref[...], sb_ref[...]
    q_scale = hd ** -0.5 * LOG2E
    for o in range(0, 2 * MOBA_WIDTH, hd):
        t = qk[:, o:o + hd]
        r = t * cos + pltpu.roll(t, hd - ROT_DIM // 2, 1) * sa + pltpu.roll(t, ROT_DIM // 2, 1) * sb
        if o < MOBA_WIDTH:
            r = r * q_scale
        pm_ref[:, o:o + hd] = r.astype(BF16)
    v = _dot(h, w_ref[:, hg_cols + 2 * MOBA_WIDTH:])
    ones_rows = jnp.ones((vt_ref.shape[2] - hd, BLK), BF16)
    for head in range(MOBA_HEADS):
        for blk in range(v.shape[0] // BLK):
            vt_ref[head, blk, :hd, :] = v[blk * BLK:(blk + 1) * BLK, head * hd:(head + 1) * hd].T.astype(BF16)
            vt_ref[head, blk, hd:, :] = ones_rows


def _inproj(xf, g, w, cos_t, sa_t, sb_t, lb, ng, cum, batch, seq):
    n = xf.shape[0]
    tm = TOKEN_TILE
    tiles_per_seq = seq // tm
    blocks_per_tile = tm // MOBA_BLOCK
    vt_rows = MOBA_HEAD_DIM + BF16_SUBLANES
    rot = lambda: pl.BlockSpec((tm, MOBA_HEAD_DIM), lambda i: (i % tiles_per_seq, 0))
    tok = lambda width: pl.BlockSpec((tm, width), lambda i: (i, 0))
    per_block = lambda rows: pl.BlockSpec(
        (None, MOBA_HEADS, blocks_per_tile, rows, MOBA_BLOCK),
        lambda i: (i // tiles_per_seq, 0, i % tiles_per_seq, 0, 0))
    hgrn_bf16 = jax.ShapeDtypeStruct((n, HGRN_WIDTH), BF16)
    return pl.pallas_call(
        _inproj_kernel,
        grid=(n // tm,),
        in_specs=[
            tok(D_MODEL),
            _const_spec((1, D_MODEL)),
            _const_spec((D_MODEL, IN_COLS)),
            rot(), rot(), rot(),
            _const_spec((1, HGRN_WIDTH)),
            _const_spec((1, HGRN_WIDTH)),
            _const_spec((HGRN_TILE, HGRN_TILE)),
        ],
        out_specs=[
            tok(HGRN_WIDTH), tok(HGRN_WIDTH), tok(HGRN_WIDTH),
            pl.BlockSpec((tm // HGRN_CHUNK, HGRN_WIDTH), lambda i: (i, 0)),
            tok(HGRN_WIDTH), tok(HGRN_WIDTH),
            tok(2 * MOBA_WIDTH),
            per_block(vt_rows),
        ],
        out_shape=[
            hgrn_bf16, hgrn_bf16, hgrn_bf16,
            jax.ShapeDtypeStruct((n // HGRN_CHUNK, HGRN_WIDTH), F32),
            hgrn_bf16, hgrn_bf16,
            jax.ShapeDtypeStruct((n, 2 * MOBA_WIDTH), BF16),
            jax.ShapeDtypeStruct((batch, MOBA_HEADS, seq // MOBA_BLOCK, vt_rows, MOBA_BLOCK), BF16),
        ],
        scratch_shapes=[pltpu.VMEM((tm, HGRN_WIDTH), F32), pltpu.VMEM((tm, HGRN_WIDTH), F32)],
        compiler_params=pltpu.CompilerParams(
            dimension_semantics=("arbitrary",), vmem_limit_bytes=VMEM_LIMIT_BYTES),
        name="inproj",
    )(xf, g, w, cos_t, sa_t, sb_t, lb, ng, cum)


def _hgrn_kernel(qt_ref, kt_ref, ke_ref, dec_ref, v_ref, gate_ref, o_ref, st_ref, keb_ref, oi_ref):
    T, C, dh = HGRN_TILE, HGRN_CHUNK, HGRN_HEAD_DIM
    n_chunks = T // C

    @pl.when((pl.program_id(0) == 0) & (pl.program_id(1) == 0))
    def _():
        keb_ref[...] = jnp.zeros_like(keb_ref)

    @pl.when(pl.program_id(1) == 0)
    def _():
        st_ref[...] = jnp.zeros_like(st_ref)

    row = lax.broadcasted_iota(jnp.int32, (T, T), 0)
    col = lax.broadcasted_iota(jnp.int32, (T, T), 1)
    intra_mask = (row // C == col // C) & (col <= row)

    for u in range(HGRN_SUBTILES):
        rows = slice(u * T, (u + 1) * T)
        for h in range(HGRN_HEADS):
            hs = slice(h * dh, (h + 1) * dh)
            for c in range(n_chunks):
                keb_ref[u, h, c * C:(c + 1) * C, c * dh:(c + 1) * dh] = ke_ref[u * T + c * C:u * T + (c + 1) * C, hs]
            vb = v_ref[rows, hs]
            a = jnp.where(intra_mask, _dot_nt(qt_ref[rows, hs], kt_ref[rows, hs]), 0.0)
            upd = _dot(vb.astype(F32).T.astype(BF16), keb_ref[u, h])
            oi_ref[u, h] = _dot(a.astype(BF16), vb)
            st = st_ref[h]
            for c in range(n_chunks):
                cs = slice(c * C, (c + 1) * C)
                oi_ref[u, h, cs, :] += _dot_nt(qt_ref[u * T + c * C:u * T + (c + 1) * C, hs],
                                               st.astype(BF16))
                st = st * dec_ref[u * n_chunks + c:u * n_chunks + c + 1, hs] + upd[:, c * dh:(c + 1) * dh]
            st_ref[h] = st
            o_h = oi_ref[u, h]
            o_h = o_h * lax.rsqrt(jnp.mean(o_h * o_h, axis=-1, keepdims=True) + NORM_EPS)
            o_ref[rows, hs] = (o_h * gate_ref[rows, hs].astype(F32)).astype(o_ref.dtype)


def _hgrn(qt, kt, ke, dec, v, gate, batch, seq):
    n = qt.shape[0]
    T, U = HGRN_TILE, HGRN_SUBTILES
    steps = seq // (T * U)
    n_chunks = T // HGRN_CHUNK
    tok = lambda: pl.BlockSpec((T * U, HGRN_WIDTH), lambda b, i: (b * steps + i, 0))
    return pl.pallas_call(
        _hgrn_kernel,
        grid=(batch, steps),
        in_specs=[
            tok(), tok(), tok(),
            pl.BlockSpec((U * n_chunks, HGRN_WIDTH), lambda b, i: (b * steps + i, 0)),
            tok(), tok(),
        ],
        out_specs=tok(),
        out_shape=jax.ShapeDtypeStruct((n, HGRN_WIDTH), BF16),
        scratch_shapes=[
            pltpu.VMEM((HGRN_HEADS, HGRN_HEAD_DIM, HGRN_HEAD_DIM), F32),
            pltpu.VMEM((U, HGRN_HEADS, T, n_chunks * HGRN_HEAD_DIM), BF16),
            pltpu.VMEM((U, HGRN_HEADS, T, HGRN_HEAD_DIM), F32),
        ],
        compiler_params=pltpu.CompilerParams(
            dimension_semantics=("arbitrary", "arbitrary"), vmem_limit_bytes=VMEM_LIMIT_BYTES),
        name="hgrn",
    )(qt, kt, ke, dec, v, gate)


def _moba_kernel(q_ref, k_ref, vt_ref, o_ref, kmean_ref, sel_ref, acc_ref, s_ref):
    BLK, hd, H = MOBA_BLOCK, MOBA_HEAD_DIM, MOBA_HEADS
    n_blocks = k_ref.shape[0] // BLK
    i = pl.program_id(1)

    @pl.when(i == 0)
    def _():
        for j in range(n_blocks):
            kb = k_ref[j * BLK:(j + 1) * BLK, :].astype(F32)
            kmean_ref[j:j + 1, :] = jnp.sum(kb, axis=0, keepdims=True) * (1.0 / BLK)

    blk_id = lax.broadcasted_iota(jnp.int32, (n_blocks, BLK), 0)
    past = blk_id < i
    k_pos = lax.broadcasted_iota(jnp.int32, (BLK, BLK), 0)
    q_pos = lax.broadcasted_iota(jnp.int32, (BLK, BLK), 1)
    causal = k_pos <= q_pos

    def scores_into(slot, j, h):
        rows = pl.ds(pl.multiple_of(jnp.minimum(j, n_blocks - 1) * BLK, BLK), BLK)
        hs = slice(h * hd, (h + 1) * hd)
        s_ref[slot, h] = _dot_nt(k_ref[rows, hs], q_ref[:, hs])

    SLOT_OWN = 4
    PAIR_A, PAIR_B = (0, 1), (2, 3)
    gates = [_dot_nt(kmean_ref[:, h * hd:(h + 1) * hd].astype(BF16), q_ref[:, h * hd:(h + 1) * hd])
             for h in range(H)]
    for h in range(H):
        scores_into(SLOT_OWN, i, h)
    for h in range(H):
        scores_into(PAIR_A[0], 0, h)
        scores_into(PAIR_A[1], 1, h)

    m_init = []
    for h in range(H):
        g = jnp.where(past, gates[h], -jnp.inf)
        chosen = jnp.zeros(g.shape, F32)
        for _ in range(MOBA_TOP_K):
            m = jnp.max(g, axis=0, keepdims=True)
            idx = jnp.min(jnp.where(g == m, blk_id, n_blocks), axis=0, keepdims=True)
            hit = blk_id == idx
            chosen = jnp.where(hit, 1.0, chosen)
            g = jnp.where(hit, -jnp.inf, g)
        sel_ref[h] = jnp.where(past, chosen, 0.0)
        s = jnp.where(causal, s_ref[SLOT_OWN, h], NEG_BIG)
        m0 = jnp.max(s, axis=0, keepdims=True)
        acc_ref[h] = _dot(vt_ref[h, i], jnp.exp2(s - m0).astype(BF16))
        m_init.append(m0)

    def consume_pair(slots, j, h, m):
        sx, sy = s_ref[slots[0], h], s_ref[slots[1], h]
        px = sel_ref[h, pl.ds(j, 1), :] > 0.0
        py = sel_ref[h, pl.ds(j + 1, 1), :] > 0.0
        m_new = jnp.maximum(m, jnp.maximum(
            jnp.where(px, jnp.max(sx, axis=0, keepdims=True), NEG_BIG),
            jnp.where(py, jnp.max(sy, axis=0, keepdims=True), NEG_BIG)))
        ex = jnp.exp2(sx - jnp.where(px, m_new, -NEG_BIG)).astype(BF16)
        ey = jnp.exp2(sy - jnp.where(py, m_new, -NEG_BIG)).astype(BF16)
        acc_ref[h] = (jnp.exp2(m - m_new) * acc_ref[h]
                      + (_dot(vt_ref[h, j], ex) + _dot(vt_ref[h, j + 1], ey)))
        return m_new

    def body(t, ms):
        ms = list(ms)
        base = 4 * t

        def produce(pair, h):
            j = base + 2 if pair == "B" else base + 4
            slots = PAIR_B if pair == "B" else PAIR_A
            scores_into(slots[0], j, h)
            scores_into(slots[1], j + 1, h)

        def consume(pair, h):
            j = base if pair == "A" else base + 2
            ms[h] = consume_pair(PAIR_A if pair == "A" else PAIR_B, j, h, ms[h])

        for op in "PB0 PB1 CA0 PB2 CA1 PB3 CA2 PA0 CA3 PA1 CB0 PA2 CB1 PA3 CB2 CB3".split():
            (produce if op[0] == "P" else consume)(op[1], int(op[2]))
        return tuple(ms)

    n_pairs = (i + 1) // 2
    ms = lax.fori_loop(0, n_pairs // 2, body, tuple(m_init))

    @pl.when(n_pairs % 2 == 1)
    def _():
        for h in range(H):
            consume_pair(PAIR_A, 2 * (n_pairs - 1), h, ms[h])

    for h in range(H):
        o = acc_ref[h, :hd, :] / acc_ref[h, hd:hd + 1, :]
        o_ref[:, h * hd:(h + 1) * hd] = o.T.astype(o_ref.dtype)


def _moba(pm, vt, batch, seq):
    n = pm.shape[0]
    BLK, hd, H = MOBA_BLOCK, MOBA_HEAD_DIM, MOBA_HEADS
    n_blocks = seq // BLK
    vt_rows = vt.shape[-2]
    return pl.pallas_call(
        _moba_kernel,
        grid=(batch, n_blocks),
        in_specs=[
            pl.BlockSpec((BLK, H * hd), lambda b, i: (b * n_blocks + i, 0)),
            pl.BlockSpec((seq, H * hd), lambda b, i: (b, 1)),
            pl.BlockSpec((None, H, n_blocks, vt_rows, BLK), lambda b, i: (b, 0, 0, 0, 0)),
        ],
        out_specs=pl.BlockSpec((BLK, H * hd), lambda b, i: (b * n_blocks + i, 0)),
        out_shape=jax.ShapeDtypeStruct((n, MOBA_WIDTH), BF16),
        scratch_shapes=[
            pltpu.VMEM((n_blocks, H * hd), F32),
            pltpu.VMEM((H, n_blocks, BLK), F32),
            pltpu.VMEM((H, vt_rows, BLK), F32),
            pltpu.VMEM((5, H, BLK, BLK), F32),
        ],
        compiler_params=pltpu.CompilerParams(
            dimension_semantics=("arbitrary", "arbitrary"),
            vmem_limit_bytes=VMEM_LIMIT_BYTES),
        name="moba",
    )(pm, pm, vt)


def _mix_xattn_kernel(x_ref, oh_ref, om_ref, wo_ref, g_ref, wq_ref, mem_ref, gm_ref, wk_ref, wv_ref,
                      wxo_ref, y_ref, k_ref, v_ref):
    @pl.when(pl.program_id(1) == 0)
    def _():
        hm = _rms(mem_ref[...], gm_ref[...]).astype(BF16)
        k_ref[...] = _dot(hm, wk_ref[...]).astype(BF16)
        v_ref[...] = _dot(hm, wv_ref[...]).astype(BF16)

    x1 = (x_ref[...] + _dot(oh_ref[...], wo_ref[:HGRN_WIDTH, :])
          + _dot(om_ref[...], wo_ref[HGRN_WIDTH:, :]))
    h = _rms(x1, g_ref[...]).astype(BF16)
    q = (_dot(h, wq_ref[...]) * (XATTN_HEAD_DIM ** -0.5 * LOG2E)).astype(BF16)
    outs = []
    for hd in range(XATTN_HEADS):
        hs = slice(hd * XATTN_HEAD_DIM, (hd + 1) * XATTN_HEAD_DIM)
        s = _dot_nt(q[:, hs], k_ref[:, hs])
        p = jnp.exp2(s - jnp.max(s, axis=-1, keepdims=True))
        l = jnp.sum(p, axis=-1, keepdims=True)
        outs.append((_dot(p.astype(BF16), v_ref[:, hs]) / l).astype(BF16))
    o = jnp.concatenate(outs, axis=-1)
    y_ref[...] = x1 + _dot(o, wxo_ref[...])


def _mix_xattn(xf, oh, om, wo, g, wq, memf, gm, wk, wv, wxo, batch, seq):
    n = xf.shape[0]
    tm = TOKEN_TILE
    steps = seq // tm
    tok = lambda w: pl.BlockSpec((tm, w), lambda b, i: (b * steps + i, 0))
    return pl.pallas_call(
        _mix_xattn_kernel,
        grid=(batch, steps),
        in_specs=[
            tok(D_MODEL), tok(HGRN_WIDTH), tok(MOBA_WIDTH),
            _const_spec((D_MODEL, D_MODEL)),
            _const_spec((1, D_MODEL)),
            _const_spec((D_MODEL, D_MODEL)),
            pl.BlockSpec((MEM_LEN, D_MODEL), lambda b, i: (b, 0)),
            _const_spec((1, D_MODEL)),
            _const_spec((D_MODEL, D_MODEL)),
            _const_spec((D_MODEL, D_MODEL)),
            _const_spec((D_MODEL, D_MODEL)),
        ],
        out_specs=tok(D_MODEL),
        out_shape=jax.ShapeDtypeStruct((n, D_MODEL), F32),
        scratch_shapes=[pltpu.VMEM((MEM_LEN, D_MODEL), BF16), pltpu.VMEM((MEM_LEN, D_MODEL), BF16)],
        compiler_params=pltpu.CompilerParams(
            dimension_semantics=("arbitrary", "arbitrary"), vmem_limit_bytes=VMEM_LIMIT_BYTES),
        name="mix_xattn",
    )(xf, oh, om, wo, g, wq, memf, gm, wk, wv, wxo)


def _mlp_kernel(x_ref, g_ref, w1_ref, w2_ref, gf_ref, y_ref, *, final_norm):
    x = x_ref[...]
    h = _rms(x, g_ref[...]).astype(BF16)
    ff_chunk = D_MODEL
    acc = x
    for c in range(D_FF // ff_chunk):
        cs = slice(c * ff_chunk, (c + 1) * ff_chunk)
        z = jnp.maximum(_dot(h, w1_ref[:, cs]), 0.0)
        acc = acc + _dot((z * z).astype(BF16), w2_ref[cs, :])
    y_ref[...] = _rms(acc, gf_ref[...]) if final_norm else acc


def _mlp(xf, g, w1, w2, gf, final_norm):
    n = xf.shape[0]
    tm = TOKEN_TILE
    return pl.pallas_call(
        functools.partial(_mlp_kernel, final_norm=final_norm),
        grid=(n // tm,),
        in_specs=[
            pl.BlockSpec((tm, D_MODEL), lambda i: (i, 0)),
            _const_spec((1, D_MODEL)),
            _const_spec((D_MODEL, D_FF)),
            _const_spec((D_FF, D_MODEL)),
            _const_spec((1, D_MODEL)),
        ],
        out_specs=pl.BlockSpec((tm, D_MODEL), lambda i: (i, 0)),
        out_shape=jax.ShapeDtypeStruct((n, D_MODEL), F32),
        compiler_params=pltpu.CompilerParams(
            dimension_semantics=("arbitrary",), vmem_limit_bytes=VMEM_LIMIT_BYTES),
        name="mlp",
    )(xf, g, w1, w2, gf)


def _rotary_tables(seq):
    half = ROT_DIM // 2
    inv_freq = np.float32(ROPE_THETA) ** (-np.arange(half, dtype=np.float32) * np.float32(2.0 / ROT_DIM))
    ang = (np.arange(seq, dtype=np.float32)[:, None] * inv_freq[None, :]).astype(np.float64)
    cos, sin = np.cos(ang), np.sin(ang)
    rest = MOBA_HEAD_DIM - ROT_DIM
    cos_t = np.concatenate([cos, cos, np.ones((seq, rest))], axis=-1)
    sa_t = np.concatenate([-sin, np.zeros((seq, MOBA_HEAD_DIM - half))], axis=-1)
    sb_t = np.concatenate([np.zeros((seq, half)), sin, np.zeros((seq, rest))], axis=-1)
    return tuple(jnp.asarray(t, F32) for t in (cos_t, sa_t, sb_t))


def _cumsum_matrix():
    T, C = HGRN_TILE, HGRN_CHUNK
    r = np.arange(T)
    same = (r[:, None] // C) == (r[None, :] // C)
    return jnp.asarray(same & (r[None, :] <= r[:, None]), BF16)


def kernel(x, mem, norm_mix, w_in, lb_logits, hgrn_norm, w_out, norm_xattn, norm_mem,
           w_xq, w_xk, w_xv, w_xo, norm_mlp, w_ff1, w_ff2, norm_final):
    batch, seq, _ = x.shape
    n = batch * seq
    xf = x.reshape(n, D_MODEL)
    lb_table = jnp.cumsum(jax.nn.softmax(lb_logits.astype(F32), axis=0), axis=0)
    cos_t, sa_t, sb_t = _rotary_tables(seq)
    cum = _cumsum_matrix()
    row = lambda v: v.reshape(1, -1)
    depth = norm_mix.shape[0]
    for l in range(depth):
        qt, kt, ke, dec, hv, gate, pm, vt = _inproj(
            xf, row(norm_mix[l]), w_in[l].astype(BF16), cos_t, sa_t, sb_t,
            row(lb_table[l]), row(hgrn_norm[l]), cum, batch, seq)
        o_hgrn = _hgrn(qt, kt, ke, dec, hv, gate, batch, seq)
        o_moba = _moba(pm, vt, batch, seq)
        x2 = _mix_xattn(xf, o_hgrn, o_moba, w_out[l].astype(BF16), row(norm_xattn[l]),
                        w_xq[l].astype(BF16), mem.reshape(-1, D_MODEL), row(norm_mem[l]),
                        w_xk[l].astype(BF16), w_xv[l].astype(BF16), w_xo[l].astype(BF16),
                        batch, seq)
        xf = _mlp(x2, row(norm_mlp[l]), w_ff1[l].astype(BF16), w_ff2[l].astype(BF16),
                  row(norm_final), final_norm=(l == depth - 1))
    return xf.reshape(batch, seq, D_MODEL)
```

```python
import functools
import math

import jax
import jax.numpy as jnp
import numpy as np
from jax import lax
from jax.experimental import pallas as pl
from jax.experimental.pallas import tpu as pltpu

F32 = jnp.float32
BF16 = jnp.bfloat16

D_MODEL = 1024
MEM_LEN = 256
HGRN_HEADS = 4
HGRN_WIDTH = 512
HGRN_HEAD_DIM = 128
HGRN_CHUNK = 32
MOBA_HEADS = 4
MOBA_WIDTH = 512
MOBA_HEAD_DIM = 128
MOBA_BLOCK = 256
MOBA_TOP_K = 3
ROPE_THETA = 500000.0
ROT_DIM = 32
XATTN_HEADS = 4
XATTN_HEAD_DIM = 256
D_FF = 4096
NORM_EPS = 1e-6
IN_COLS = 4 * HGRN_WIDTH + 3 * MOBA_WIDTH

LOG2E = math.log2(math.e)
NEG_BIG = -1e30

VMEM_LIMIT_BYTES = 56 * 1024 * 1024

BF16_SUBLANES = 16
HGRN_TILE = 256
HGRN_SUBTILES = 4
TOKEN_TILE = 1024


def _rms(x, g):
    return x * lax.rsqrt(jnp.mean(x * x, axis=-1, keepdims=True) + NORM_EPS) * g


def _dot(a, b):
    return jnp.dot(a, b, preferred_element_type=F32)


def _dot_nt(a, b):
    return lax.dot_general(a, b, (((1,), (1,)), ((), ())), preferred_element_type=F32)


def _const_spec(shape):
    zeros = (0,) * len(shape)
    return pl.BlockSpec(shape, lambda *_: zeros, pipeline_mode=pl.Buffered(1))


def _inproj_kernel(x_ref, g_ref, w_ref, cos_ref, sa_ref, sb_ref, lb_ref, ng_ref, cum_ref,
                   qt_ref, kt_ref, ke_ref, dec_ref, v_ref, gate_ref, pm_ref, vt_ref, b_ref, k32_ref):
    h = _rms(x_ref[...], g_ref[...]).astype(BF16)
    W, C, T = HGRN_WIDTH, HGRN_CHUNK, HGRN_TILE
    hg_cols = 4 * W
    hd, BLK = MOBA_HEAD_DIM, MOBA_BLOCK
    tm = x_ref.shape[0]

    lb = lb_ref[...]
    hf = _dot(h, w_ref[:, W:2 * W])
    hq = _dot(h, w_ref[:, 0:W])
    hg = _dot(h, w_ref[:, 3 * W:4 * W])
    v_ref[...] = _dot(h, w_ref[:, 2 * W:3 * W]).astype(BF16)
    f = 0.5 * (1.0 + lb) + (0.5 * (1.0 - lb)) * jnp.tanh(0.5 * hf)
    logf = jnp.log2(f)
    l1 = logf.astype(BF16)
    l2 = (logf - l1.astype(F32)).astype(BF16)
    cum = cum_ref[...]
    for t in range(tm // T):
        rows = slice(t * T, (t + 1) * T)
        b_ref[rows, :] = _dot(cum, l1[rows, :]) + _dot(cum, l2[rows, :])
    b = b_ref[...]
    qt_ref[...] = (hq * jnp.exp2(b)).astype(BF16)
    k32 = (1.0 - f) * jnp.exp2(-b)
    kt_ref[...] = k32.astype(BF16)
    k32_ref[...] = k32
    for c in range(tm // C):
        cs = slice(c * C, (c + 1) * C)
        decay = jnp.exp2(b_ref[(c + 1) * C - 1:(c + 1) * C, :])
        dec_ref[c:c + 1, :] = decay
        ke_ref[cs, :] = (k32_ref[cs, :] * decay).astype(BF16)
    gate_ref[...] = (ng_ref[...] * (0.5 * hg * (1.0 + jnp.tanh(0.5 * hg)))).astype(BF16)

    qk = _dot(h, w_ref[:, hg_cols:hg_cols + 2 * MOBA_WIDTH])
    cos, sa, sb = cos_ref[...], sa_ref[...], sb_ref[...]
    q_scale = hd ** -0.5 * LOG2E
    for o in range(0, 2 * MOBA_WIDTH, hd):
        t = qk[:, o:o + hd]
        r = t * cos + pltpu.roll(t, hd - ROT_DIM // 2, 1) * sa + pltpu.roll(t, ROT_DIM // 2, 1) * sb
        if o < MOBA_WIDTH:
            r = r * q_scale
        pm_ref[:, o:o + hd] = r.astype(BF16)
    v = _dot(h, w_ref[:, hg_cols + 2 * MOBA_WIDTH:])
    ones_rows = jnp.ones((vt_ref.shape[2] - hd, BLK), BF16)
    for head in range(MOBA_HEADS):
        for blk in range(v.shape[0] // BLK):
            vt_ref[head, blk, :hd, :] = v[blk * BLK:(blk + 1) * BLK, head * hd:(head + 1) * hd].T.astype(BF16)
            vt_ref[head, blk, hd:, :] = ones_rows


def _inproj(xf, g, w, cos_t, sa_t, sb_t, lb, ng, cum, batch, seq):
    n = xf.shape[0]
    tm = TOKEN_TILE
    tiles_per_seq = seq // tm
    blocks_per_tile = tm // MOBA_BLOCK
    vt_rows = MOBA_HEAD_DIM + BF16_SUBLANES
    rot = lambda: pl.BlockSpec((tm, MOBA_HEAD_DIM), lambda i: (i % tiles_per_seq, 0))
    tok = lambda width: pl.BlockSpec((tm, width), lambda i: (i, 0))
    per_block = lambda rows: pl.BlockSpec(
        (None, MOBA_HEADS, blocks_per_tile, rows, MOBA_BLOCK),
        lambda i: (i // tiles_per_seq, 0, i % tiles_per_seq, 0, 0))
    hgrn_bf16 = jax.ShapeDtypeStruct((n, HGRN_WIDTH), BF16)
    return pl.pallas_call(
        _inproj_kernel,
        grid=(n // tm,),
        in_specs=[
            tok(D_MODEL),
            _const_spec((1, D_MODEL)),
            _const_spec((D_MODEL, IN_COLS)),
            rot(), rot(), rot(),
            _const_spec((1, HGRN_WIDTH)),
            _const_spec((1, HGRN_WIDTH)),
            _const_spec((HGRN_TILE, HGRN_TILE)),
        ],
        out_specs=[
            tok(HGRN_WIDTH), tok(HGRN_WIDTH), tok(HGRN_WIDTH),
            pl.BlockSpec((tm // HGRN_CHUNK, HGRN_WIDTH), lambda i: (i, 0)),
            tok(HGRN_WIDTH), tok(HGRN_WIDTH),
            tok(2 * MOBA_WIDTH),
            per_block(vt_rows),
        ],
        out_shape=[
            hgrn_bf16, hgrn_bf16, hgrn_bf16,
            jax.ShapeDtypeStruct((n // HGRN_CHUNK, HGRN_WIDTH), F32),
            hgrn_bf16, hgrn_bf16,
            jax.ShapeDtypeStruct((n, 2 * MOBA_WIDTH), BF16),
            jax.ShapeDtypeStruct((batch, MOBA_HEADS, seq // MOBA_BLOCK, vt_rows, MOBA_BLOCK), BF16),
        ],
        scratch_shapes=[pltpu.VMEM((tm, HGRN_WIDTH), F32), pltpu.VMEM((tm, HGRN_WIDTH), F32)],
        compiler_params=pltpu.CompilerParams(
            dimension_semantics=("arbitrary",), vmem_limit_bytes=VMEM_LIMIT_BYTES),
        name="inproj",
    )(xf, g, w, cos_t, sa_t, sb_t, lb, ng, cum)


def _hgrn_kernel(qt_ref, kt_ref, ke_ref, dec_ref, v_ref, gate_ref, *rest, n_cast):
    w_refs, o_ref, wb_refs = rest[:n_cast], rest[n_cast], rest[n_cast + 1:2 * n_cast + 1]
    st_ref, keb_ref, oi_ref = rest[2 * n_cast + 1:]
    for w_ref, wb_ref in zip(w_refs, wb_refs):
        wb_ref[...] = w_ref[...].astype(BF16)
    T, C, dh = HGRN_TILE, HGRN_CHUNK, HGRN_HEAD_DIM
    n_chunks = T // C

    @pl.when((pl.program_id(0) == 0) & (pl.program_id(1) == 0))
    def _():
        keb_ref[...] = jnp.zeros_like(keb_ref)

    @pl.when(pl.program_id(1) == 0)
    def _():
        st_ref[...] = jnp.zeros_like(st_ref)

    row = lax.broadcasted_iota(jnp.int32, (T, T), 0)
    col = lax.broadcasted_iota(jnp.int32, (T, T), 1)
    intra_mask = (row // C == col // C) & (col <= row)

    for u in range(HGRN_SUBTILES):
        rows = slice(u * T, (u + 1) * T)
        for h in range(HGRN_HEADS):
            hs = slice(h * dh, (h + 1) * dh)
            for c in range(n_chunks):
                keb_ref[u, h, c * C:(c + 1) * C, c * dh:(c + 1) * dh] = ke_ref[u * T + c * C:u * T + (c + 1) * C, hs]
            vb = v_ref[rows, hs]
            a = jnp.where(intra_mask, _dot_nt(qt_ref[rows, hs], kt_ref[rows, hs]), 0.0)
            upd = _dot(vb.astype(F32).T.astype(BF16), keb_ref[u, h])
            oi_ref[u, h] = _dot(a.astype(BF16), vb)
            st = st_ref[h]
            for c in range(n_chunks):
                cs = slice(c * C, (c + 1) * C)
                oi_ref[u, h, cs, :] += _dot_nt(qt_ref[u * T + c * C:u * T + (c + 1) * C, hs],
                                               st.astype(BF16))
                st = st * dec_ref[u * n_chunks + c:u * n_chunks + c + 1, hs] + upd[:, c * dh:(c + 1) * dh]
            st_ref[h] = st
            o_h = oi_ref[u, h]
            o_h = o_h * lax.rsqrt(jnp.mean(o_h * o_h, axis=-1, keepdims=True) + NORM_EPS)
            o_ref[rows, hs] = (o_h * gate_ref[rows, hs].astype(F32)).astype(o_ref.dtype)


def _hgrn(qt, kt, ke, dec, v, gate, weights, batch, seq):
    n = qt.shape[0]
    T, U = HGRN_TILE, HGRN_SUBTILES
    steps = seq // (T * U)
    n_chunks = T // HGRN_CHUNK
    tok = lambda: pl.BlockSpec((T * U, HGRN_WIDTH), lambda b, i: (b * steps + i, 0))
    row_block = lambda w: pl.BlockSpec((w.shape[0] // (batch * steps), w.shape[1]),
                                       lambda b, i: (b * steps + i, 0))
    outs = pl.pallas_call(
        functools.partial(_hgrn_kernel, n_cast=len(weights)),
        grid=(batch, steps),
        in_specs=[
            tok(), tok(), tok(),
            pl.BlockSpec((U * n_chunks, HGRN_WIDTH), lambda b, i: (b * steps + i, 0)),
            tok(), tok(),
        ] + [row_block(w) for w in weights],
        out_specs=[tok()] + [row_block(w) for w in weights],
        out_shape=[jax.ShapeDtypeStruct((n, HGRN_WIDTH), BF16)]
        + [jax.ShapeDtypeStruct(w.shape, BF16) for w in weights],
        scratch_shapes=[
            pltpu.VMEM((HGRN_HEADS, HGRN_HEAD_DIM, HGRN_HEAD_DIM), F32),
            pltpu.VMEM((U, HGRN_HEADS, T, n_chunks * HGRN_HEAD_DIM), BF16),
            pltpu.VMEM((U, HGRN_HEADS, T, HGRN_HEAD_DIM), F32),
        ],
        compiler_params=pltpu.CompilerParams(
            dimension_semantics=("arbitrary", "arbitrary"), vmem_limit_bytes=VMEM_LIMIT_BYTES),
        name="hgrn",
    )(qt, kt, ke, dec, v, gate, *weights)
    return outs[0], outs[1:]


def _moba_kernel(q_ref, k_ref, vt_ref, o_ref, kmean_ref, sel_ref, acc_ref, s_ref):
    BLK, hd, H = MOBA_BLOCK, MOBA_HEAD_DIM, MOBA_HEADS
    n_blocks = k_ref.shape[0] // BLK
    i = pl.program_id(1)

    @pl.when(i == 0)
    def _():
        for j in range(n_blocks):
            kb = k_ref[j * BLK:(j + 1) * BLK, :].astype(F32)
            kmean_ref[j:j + 1, :] = jnp.sum(kb, axis=0, keepdims=True) * (1.0 / BLK)

    blk_id = lax.broadcasted_iota(jnp.int32, (n_blocks, BLK), 0)
    past = blk_id < i
    k_pos = lax.broadcasted_iota(jnp.int32, (BLK, BLK), 0)
    q_pos = lax.broadcasted_iota(jnp.int32, (BLK, BLK), 1)
    causal = k_pos <= q_pos

    def scores_into(slot, j, h):
        rows = pl.ds(pl.multiple_of(jnp.minimum(j, n_blocks - 1) * BLK, BLK), BLK)
        hs = slice(h * hd, (h + 1) * hd)
        s_ref[slot, h] = _dot_nt(k_ref[rows, hs], q_ref[:, hs])

    SLOT_OWN = 4
    PAIR_A, PAIR_B = (0, 1), (2, 3)
    gates = [_dot_nt(kmean_ref[:, h * hd:(h + 1) * hd].astype(BF16), q_ref[:, h * hd:(h + 1) * hd])
             for h in range(H)]
    for h in range(H):
        scores_into(SLOT_OWN, i, h)
    for h in range(H):
        scores_into(PAIR_A[0], 0, h)
        scores_into(PAIR_A[1], 1, h)

    m_init = []
    for h in range(H):
        g = jnp.where(past, gates[h], -jnp.inf)
        chosen = jnp.zeros(g.shape, F32)
        for _ in range(MOBA_TOP_K):
            m = jnp.max(g, axis=0, keepdims=True)
            idx = jnp.min(jnp.where(g == m, blk_id, n_blocks), axis=0, keepdims=True)
            hit = blk_id == idx
            chosen = jnp.where(hit, 1.0, chosen)
            g = jnp.where(hit, -jnp.inf, g)
        sel_ref[h] = jnp.where(past, chosen, 0.0)
        s = jnp.where(causal, s_ref[SLOT_OWN, h], NEG_BIG)
        m0 = jnp.max(s, axis=0, keepdims=True)
        acc_ref[h] = _dot(vt_ref[h, i], jnp.exp2(s - m0).astype(BF16))
        m_init.append(m0)

    def consume_pair(slots, j, h, m):
        sx, sy = s_ref[slots[0], h], s_ref[slots[1], h]
        px = sel_ref[h, pl.ds(j, 1), :] > 0.0
        py = sel_ref[h, pl.ds(j + 1, 1), :] > 0.0
        m_new = jnp.maximum(m, jnp.maximum(
            jnp.where(px, jnp.max(sx, axis=0, keepdims=True), NEG_BIG),
            jnp.where(py, jnp.max(sy, axis=0, keepdims=True), NEG_BIG)))
        ex = jnp.exp2(sx - jnp.where(px, m_new, -NEG_BIG)).astype(BF16)
        ey = jnp.exp2(sy - jnp.where(py, m_new, -NEG_BIG)).astype(BF16)
        acc_ref[h] = (jnp.exp2(m - m_new) * acc_ref[h]
                      + (_dot(vt_ref[h, j], ex) + _dot(vt_ref[h, j + 1], ey)))
        return m_new

    def body(t, ms):
        ms = list(ms)
        base = 4 * t

        def produce(pair, h):
            j = base + 2 if pair == "B" else base + 4
            slots = PAIR_B if pair == "B" else PAIR_A
            scores_into(slots[0], j, h)
            scores_into(slots[1], j + 1, h)

        def consume(pair, h):
            j = base if pair == "A" else base + 2
            ms[h] = consume_pair(PAIR_A if pair == "A" else PAIR_B, j, h, ms[h])

        for op in "PB0 PB1 CA0 PB2 CA1 PB3 CA2 PA0 CA3 PA1 CB0 PA2 CB1 PA3 CB2 CB3".split():
            (produce if op[0] == "P" else consume)(op[1], int(op[2]))
        return tuple(ms)

    n_pairs = (i + 1) // 2
    ms = lax.fori_loop(0, n_pairs // 2, body, tuple(m_init))

    @pl.when(n_pairs % 2 == 1)
    def _():
        for h in range(H):
            consume_pair(PAIR_A, 2 * (n_pairs - 1), h, ms[h])

    for h in range(H):
        o = acc_ref[h, :hd, :] / acc_ref[h, hd:hd + 1, :]
        o_ref[:, h * hd:(h + 1) * hd] = o.T.astype(o_ref.dtype)


def _moba(pm, vt, batch, seq):
    n = pm.shape[0]
    BLK, hd, H = MOBA_BLOCK, MOBA_HEAD_DIM, MOBA_HEADS
    n_blocks = seq // BLK
    vt_rows = vt.shape[-2]
    return pl.pallas_call(
        _moba_kernel,
        grid=(batch, n_blocks),
        in_specs=[
            pl.BlockSpec((BLK, H * hd), lambda b, i: (b * n_blocks + i, 0)),
            pl.BlockSpec((seq, H * hd), lambda b, i: (b, 1)),
            pl.BlockSpec((None, H, n_blocks, vt_rows, BLK), lambda b, i: (b, 0, 0, 0, 0)),
        ],
        out_specs=pl.BlockSpec((BLK, H * hd), lambda b, i: (b * n_blocks + i, 0)),
        out_shape=jax.ShapeDtypeStruct((n, MOBA_WIDTH), BF16),
        scratch_shapes=[
            pltpu.VMEM((n_blocks, H * hd), F32),
            pltpu.VMEM((H, n_blocks, BLK), F32),
            pltpu.VMEM((H, vt_rows, BLK), F32),
            pltpu.VMEM((5, H, BLK, BLK), F32),
        ],
        compiler_params=pltpu.CompilerParams(
            dimension_semantics=("arbitrary", "arbitrary"),
            vmem_limit_bytes=VMEM_LIMIT_BYTES),
        name="moba",
    )(pm, pm, vt)


def _mix_xattn_kernel(x_ref, oh_ref, om_ref, wo_ref, g_ref, wq_ref, mem_ref, gm_ref, wk_ref, wv_ref,
                      wxo_ref, y_ref, k_ref, v_ref):
    @pl.when(pl.program_id(1) == 0)
    def _():
        hm = _rms(mem_ref[...], gm_ref[...]).astype(BF16)
        k_ref[...] = _dot(hm, wk_ref[...]).astype(BF16)
        v_ref[...] = _dot(hm, wv_ref[...]).astype(BF16)

    x1 = (x_ref[...] + _dot(oh_ref[...], wo_ref[:HGRN_WIDTH, :])
          + _dot(om_ref[...], wo_ref[HGRN_WIDTH:, :]))
    h = _rms(x1, g_ref[...]).astype(BF16)
    q = (_dot(h, wq_ref[...]) * (XATTN_HEAD_DIM ** -0.5 * LOG2E)).astype(BF16)
    outs = []
    for hd in range(XATTN_HEADS):
        hs = slice(hd * XATTN_HEAD_DIM, (hd + 1) * XATTN_HEAD_DIM)
        s = _dot_nt(q[:, hs], k_ref[:, hs])
        p = jnp.exp2(s - jnp.max(s, axis=-1, keepdims=True))
        l = jnp.sum(p, axis=-1, keepdims=True)
        outs.append((_dot(p.astype(BF16), v_ref[:, hs]) / l).astype(BF16))
    o = jnp.concatenate(outs, axis=-1)
    y_ref[...] = x1 + _dot(o, wxo_ref[...])


def _mix_xattn(xf, oh, om, wo, g, wq, memf, gm, wk, wv, wxo, batch, seq):
    n = xf.shape[0]
    tm = TOKEN_TILE
    steps = seq // tm
    tok = lambda w: pl.BlockSpec((tm, w), lambda b, i: (b * steps + i, 0))
    return pl.pallas_call(
        _mix_xattn_kernel,
        grid=(batch, steps),
        in_specs=[
            tok(D_MODEL), tok(HGRN_WIDTH), tok(MOBA_WIDTH),
            _const_spec((D_MODEL, D_MODEL)),
            _const_spec((1, D_MODEL)),
            _const_spec((D_MODEL, D_MODEL)),
            pl.BlockSpec((MEM_LEN, D_MODEL), lambda b, i: (b, 0)),
            _const_spec((1, D_MODEL)),
            _const_spec((D_MODEL, D_MODEL)),
            _const_spec((D_MODEL, D_MODEL)),
            _const_spec((D_MODEL, D_MODEL)),
        ],
        out_specs=tok(D_MODEL),
        out_shape=jax.ShapeDtypeStruct((n, D_MODEL), F32),
        scratch_shapes=[pltpu.VMEM((MEM_LEN, D_MODEL), BF16), pltpu.VMEM((MEM_LEN, D_MODEL), BF16)],
        compiler_params=pltpu.CompilerParams(
            dimension_semantics=("arbitrary", "arbitrary"), vmem_limit_bytes=VMEM_LIMIT_BYTES),
        name="mix_xattn",
    )(xf, oh, om, wo, g, wq, memf, gm, wk, wv, wxo)


def _mlp_kernel(x_ref, g_ref, w1_ref, w2_ref, gf_ref, y_ref, *, final_norm):
    x = x_ref[...]
    h = _rms(x, g_ref[...]).astype(BF16)
    ff_chunk = D_MODEL
    acc = x
    for c in range(D_FF // ff_chunk):
        cs = slice(c * ff_chunk, (c + 1) * ff_chunk)
        z = jnp.maximum(_dot(h, w1_ref[:, cs]), 0.0)
        acc = acc + _dot((z * z).astype(BF16), w2_ref[cs, :])
    y_ref[...] = _rms(acc, gf_ref[...]) if final_norm else acc


def _mlp(xf, g, w1, w2, gf, final_norm):
    n = xf.shape[0]
    tm = TOKEN_TILE
    return pl.pallas_call(
        functools.partial(_mlp_kernel, final_norm=final_norm),
        grid=(n // tm,),
        in_specs=[
            pl.BlockSpec((tm, D_MODEL), lambda i: (i, 0)),
            _const_spec((1, D_MODEL)),
            _const_spec((D_MODEL, D_FF)),
            _const_spec((D_FF, D_MODEL)),
            _const_spec((1, D_MODEL)),
        ],
        out_specs=pl.BlockSpec((tm, D_MODEL), lambda i: (i, 0)),
        out_shape=jax.ShapeDtypeStruct((n, D_MODEL), F32),
        compiler_params=pltpu.CompilerParams(
            dimension_semantics=("arbitrary",), vmem_limit_bytes=VMEM_LIMIT_BYTES),
        name="mlp",
    )(xf, g, w1, w2, gf)


def _rotary_tables(seq):
    half = ROT_DIM // 2
    inv_freq = np.float32(ROPE_THETA) ** (-np.arange(half, dtype=np.float32) * np.float32(2.0 / ROT_DIM))
    ang = (np.arange(seq, dtype=np.float32)[:, None] * inv_freq[None, :]).astype(np.float64)
    cos, sin = np.cos(ang), np.sin(ang)
    rest = MOBA_HEAD_DIM - ROT_DIM
    cos_t = np.concatenate([cos, cos, np.ones((seq, rest))], axis=-1)
    sa_t = np.concatenate([-sin, np.zeros((seq, MOBA_HEAD_DIM - half))], axis=-1)
    sb_t = np.concatenate([np.zeros((seq, half)), sin, np.zeros((seq, rest))], axis=-1)
    return tuple(jnp.asarray(t, F32) for t in (cos_t, sa_t, sb_t))


def _cumsum_matrix():
    T, C = HGRN_TILE, HGRN_CHUNK
    r = np.arange(T)
    same = (r[:, None] // C) == (r[None, :] // C)
    return jnp.asarray(same & (r[None, :] <= r[:, None]), BF16)


def kernel(x, mem, norm_mix, w_in, lb_logits, hgrn_norm, w_out, norm_xattn, norm_mem,
           w_xq, w_xk, w_xv, w_xo, norm_mlp, w_ff1, w_ff2, norm_final):
    batch, seq, _ = x.shape
    n = batch * seq
    xf = x.reshape(n, D_MODEL)
    lb_table = jnp.cumsum(jax.nn.softmax(lb_logits.astype(F32), axis=0), axis=0)
    cos_t, sa_t, sb_t = _rotary_tables(seq)
    cum = _cumsum_matrix()
    row = lambda v: v.reshape(1, -1)
    depth = norm_mix.shape[0]
    for l in range(depth):
        qt, kt, ke, dec, hv, gate, pm, vt = _inproj(
            xf, row(norm_mix[l]), w_in[l].astype(BF16), cos_t, sa_t, sb_t,
            row(lb_table[l]), row(hgrn_norm[l]), cum, batch, seq)
        o_hgrn, (wo, wq, wk, wv, wxo, w1, w2) = _hgrn(
            qt, kt, ke, dec, hv, gate,
            (w_out[l], w_xq[l], w_xk[l], w_xv[l], w_xo[l], w_ff1[l], w_ff2[l]), batch, seq)
        o_moba = _moba(pm, vt, batch, seq)
        x2 = _mix_xattn(xf, o_hgrn, o_moba, wo, row(norm_xattn[l]), wq,
                        mem.reshape(-1, D_MODEL), row(norm_mem[l]), wk, wv, wxo, batch, seq)
        xf = _mlp(x2, row(norm_mlp[l]), w1, w2, row(norm_final), final_norm=(l == depth - 1))
    return xf.reshape(batch, seq, D_MODEL)
```

```python
import functools
import math

import jax
import jax.numpy as jnp
import numpy as np
from jax import lax
from jax.experimental import pallas as pl
from jax.experimental.pallas import tpu as pltpu

F32 = jnp.float32
BF16 = jnp.bfloat16

D_MODEL = 1024
MEM_LEN = 256
HGRN_HEADS = 4
HGRN_WIDTH = 512
HGRN_HEAD_DIM = 128
HGRN_CHUNK = 32
MOBA_HEADS = 4
MOBA_WIDTH = 512
MOBA_HEAD_DIM = 128
MOBA_BLOCK = 256
MOBA_TOP_K = 3
MOBA_QUERY_BLOCKS = 2
ROPE_THETA = 500000.0
ROT_DIM = 32
XATTN_HEADS = 4
XATTN_HEAD_DIM = 256
D_FF = 4096
NORM_EPS = 1e-6
IN_COLS = 4 * HGRN_WIDTH + 3 * MOBA_WIDTH

LOG2E = math.log2(math.e)
NEG_BIG = -1e30

VMEM_LIMIT_BYTES = 56 * 1024 * 1024

BF16_SUBLANES = 16
HGRN_TILE = 256
HGRN_SUBTILES = 4
TOKEN_TILE = 1024


def _rms(x, g):
    return x * lax.rsqrt(jnp.mean(x * x, axis=-1, keepdims=True) + NORM_EPS) * g


def _dot(a, b):
    return jnp.dot(a, b, preferred_element_type=F32)


def _dot_nt(a, b):
    return lax.dot_general(a, b, (((1,), (1,)), ((), ())), preferred_element_type=F32)


def _const_spec(shape):
    zeros = (0,) * len(shape)
    return pl.BlockSpec(shape, lambda *_: zeros, pipeline_mode=pl.Buffered(1))


def _inproj_kernel(x_ref, g_ref, w_ref, cos_ref, sa_ref, sb_ref, lb_ref, ng_ref, cum_ref,
                   qt_ref, kt_ref, ke_ref, dec_ref, v_ref, gate_ref, pm_ref, vt_ref, b_ref, k32_ref):
    h = _rms(x_ref[...], g_ref[...]).astype(BF16)
    W, C, T = HGRN_WIDTH, HGRN_CHUNK, HGRN_TILE
    hg_cols = 4 * W
    hd, BLK = MOBA_HEAD_DIM, MOBA_BLOCK
    tm = x_ref.shape[0]

    lb = lb_ref[...]
    hf = _dot(h, w_ref[:, W:2 * W])
    hq = _dot(h, w_ref[:, 0:W])
    hg = _dot(h, w_ref[:, 3 * W:4 * W])
    v_ref[...] = _dot(h, w_ref[:, 2 * W:3 * W]).astype(BF16)
    f = 0.5 * (1.0 + lb) + (0.5 * (1.0 - lb)) * jnp.tanh(0.5 * hf)
    logf = jnp.log2(f)
    l1 = logf.astype(BF16)
    l2 = (logf - l1.astype(F32)).astype(BF16)
    cum = cum_ref[...]
    for t in range(tm // T):
        rows = slice(t * T, (t + 1) * T)
        b_ref[rows, :] = _dot(cum, l1[rows, :]) + _dot(cum, l2[rows, :])
    b = b_ref[...]
    qt_ref[...] = (hq * jnp.exp2(b)).astype(BF16)
    k32 = (1.0 - f) * jnp.exp2(-b)
    kt_ref[...] = k32.astype(BF16)
    k32_ref[...] = k32
    for c in range(tm // C):
        cs = slice(c * C, (c + 1) * C)
        decay = jnp.exp2(b_ref[(c + 1) * C - 1:(c + 1) * C, :])
        dec_ref[c:c + 1, :] = decay
        ke_ref[cs, :] = (k32_ref[cs, :] * decay).astype(BF16)
    gate_ref[...] = (ng_ref[...] * (0.5 * hg * (1.0 + jnp.tanh(0.5 * hg)))).astype(BF16)

    qk = _dot(h, w_ref[:, hg_cols:hg_cols + 2 * MOBA_WIDTH])
    cos, sa, sb = cos_ref[...], sa_ref[...], sb_ref[...]
    q_scale = hd ** -0.5 * LOG2E
    for o in range(0, 2 * MOBA_WIDTH, hd):
        t = qk[:, o:o + hd]
        r = t * cos + pltpu.roll(t, hd - ROT_DIM // 2, 1) * sa + pltpu.roll(t, ROT_DIM // 2, 1) * sb
        if o < MOBA_WIDTH:
            r = r * q_scale
        pm_ref[:, o:o + hd] = r.astype(BF16)
    v = _dot(h, w_ref[:, hg_cols + 2 * MOBA_WIDTH:])
    ones_rows = jnp.ones((vt_ref.shape[2] - hd, BLK), BF16)
    for head in range(MOBA_HEADS):
        for blk in range(v.shape[0] // BLK):
            vt_ref[head, blk, :hd, :] = v[blk * BLK:(blk + 1) * BLK, head * hd:(head + 1) * hd].T.astype(BF16)
            vt_ref[head, blk, hd:, :] = ones_rows


def _inproj(xf, g, w, cos_t, sa_t, sb_t, lb, ng, cum, batch, seq):
    n = xf.shape[0]
    tm = TOKEN_TILE
    tiles_per_seq = seq // tm
    blocks_per_tile = tm // MOBA_BLOCK
    vt_rows = MOBA_HEAD_DIM + BF16_SUBLANES
    rot = lambda: pl.BlockSpec((tm, MOBA_HEAD_DIM), lambda i: (i % tiles_per_seq, 0))
    tok = lambda width: pl.BlockSpec((tm, width), lambda i: (i, 0))
    per_block = lambda rows: pl.BlockSpec(
        (None, MOBA_HEADS, blocks_per_tile, rows, MOBA_BLOCK),
        lambda i: (i // tiles_per_seq, 0, i % tiles_per_seq, 0, 0))
    hgrn_bf16 = jax.ShapeDtypeStruct((n, HGRN_WIDTH), BF16)
    return pl.pallas_call(
        _inproj_kernel,
        grid=(n // tm,),
        in_specs=[
            tok(D_MODEL),
            _const_spec((1, D_MODEL)),
            _const_spec((D_MODEL, IN_COLS)),
            rot(), rot(), rot(),
            _const_spec((1, HGRN_WIDTH)),
            _const_spec((1, HGRN_WIDTH)),
            _const_spec((HGRN_TILE, HGRN_TILE)),
        ],
        out_specs=[
            tok(HGRN_WIDTH), tok(HGRN_WIDTH), tok(HGRN_WIDTH),
            pl.BlockSpec((tm // HGRN_CHUNK, HGRN_WIDTH), lambda i: (i, 0)),
            tok(HGRN_WIDTH), tok(HGRN_WIDTH),
            tok(2 * MOBA_WIDTH),
            per_block(vt_rows),
        ],
        out_shape=[
            hgrn_bf16, hgrn_bf16, hgrn_bf16,
            jax.ShapeDtypeStruct((n // HGRN_CHUNK, HGRN_WIDTH), F32),
            hgrn_bf16, hgrn_bf16,
            jax.ShapeDtypeStruct((n, 2 * MOBA_WIDTH), BF16),
            jax.ShapeDtypeStruct((batch, MOBA_HEADS, seq // MOBA_BLOCK, vt_rows, MOBA_BLOCK), BF16),
        ],
        scratch_shapes=[pltpu.VMEM((tm, HGRN_WIDTH), F32), pltpu.VMEM((tm, HGRN_WIDTH), F32)],
        compiler_params=pltpu.CompilerParams(
            dimension_semantics=("arbitrary",), vmem_limit_bytes=VMEM_LIMIT_BYTES),
        name="inproj",
    )(xf, g, w, cos_t, sa_t, sb_t, lb, ng, cum)


def _hgrn_kernel(qt_ref, kt_ref, ke_ref, dec_ref, v_ref, gate_ref, *rest, n_cast):
    w_refs, o_ref, wb_refs = rest[:n_cast], rest[n_cast], rest[n_cast + 1:2 * n_cast + 1]
    st_ref, keb_ref, oi_ref = rest[2 * n_cast + 1:]
    for w_ref, wb_ref in zip(w_refs, wb_refs):
        wb_ref[...] = w_ref[...].astype(BF16)
    T, C, dh = HGRN_TILE, HGRN_CHUNK, HGRN_HEAD_DIM
    n_chunks = T // C

    @pl.when((pl.program_id(0) == 0) & (pl.program_id(1) == 0))
    def _():
        keb_ref[...] = jnp.zeros_like(keb_ref)

    @pl.when(pl.program_id(1) == 0)
    def _():
        st_ref[...] = jnp.zeros_like(st_ref)

    row = lax.broadcasted_iota(jnp.int32, (T, T), 0)
    col = lax.broadcasted_iota(jnp.int32, (T, T), 1)
    intra_mask = (row // C == col // C) & (col <= row)

    for u in range(HGRN_SUBTILES):
        rows = slice(u * T, (u + 1) * T)
        for h in range(HGRN_HEADS):
            hs = slice(h * dh, (h + 1) * dh)
            for c in range(n_chunks):
                keb_ref[u, h, c * C:(c + 1) * C, c * dh:(c + 1) * dh] = ke_ref[u * T + c * C:u * T + (c + 1) * C, hs]
            vb = v_ref[rows, hs]
            a = jnp.where(intra_mask, _dot_nt(qt_ref[rows, hs], kt_ref[rows, hs]), 0.0)
            upd = _dot(vb.astype(F32).T.astype(BF16), keb_ref[u, h])
            oi_ref[u, h] = _dot(a.astype(BF16), vb)
            st = st_ref[h]
            for c in range(n_chunks):
                cs = slice(c * C, (c + 1) * C)
                oi_ref[u, h, cs, :] += _dot_nt(qt_ref[u * T + c * C:u * T + (c + 1) * C, hs],
                                               st.astype(BF16))
                st = st * dec_ref[u * n_chunks + c:u * n_chunks + c + 1, hs] + upd[:, c * dh:(c + 1) * dh]
            st_ref[h] = st
            o_h = oi_ref[u, h]
            o_h = o_h * lax.rsqrt(jnp.mean(o_h * o_h, axis=-1, keepdims=True) + NORM_EPS)
            o_ref[rows, hs] = (o_h * gate_ref[rows, hs].astype(F32)).astype(o_ref.dtype)


def _hgrn(qt, kt, ke, dec, v, gate, weights, batch, seq):
    n = qt.shape[0]
    T, U = HGRN_TILE, HGRN_SUBTILES
    steps = seq // (T * U)
    n_chunks = T // HGRN_CHUNK
    tok = lambda: pl.BlockSpec((T * U, HGRN_WIDTH), lambda b, i: (b * steps + i, 0))
    row_block = lambda w: pl.BlockSpec((w.shape[0] // (batch * steps), w.shape[1]),
                                       lambda b, i: (b * steps + i, 0))
    outs = pl.pallas_call(
        functools.partial(_hgrn_kernel, n_cast=len(weights)),
        grid=(batch, steps),
        in_specs=[
            tok(), tok(), tok(),
            pl.BlockSpec((U * n_chunks, HGRN_WIDTH), lambda b, i: (b * steps + i, 0)),
            tok(), tok(),
        ] + [row_block(w) for w in weights],
        out_specs=[tok()] + [row_block(w) for w in weights],
        out_shape=[jax.ShapeDtypeStruct((n, HGRN_WIDTH), BF16)]
        + [jax.ShapeDtypeStruct(w.shape, BF16) for w in weights],
        scratch_shapes=[
            pltpu.VMEM((HGRN_HEADS, HGRN_HEAD_DIM, HGRN_HEAD_DIM), F32),
            pltpu.VMEM((U, HGRN_HEADS, T, n_chunks * HGRN_HEAD_DIM), BF16),
            pltpu.VMEM((U, HGRN_HEADS, T, HGRN_HEAD_DIM), F32),
        ],
        compiler_params=pltpu.CompilerParams(
            dimension_semantics=("arbitrary", "arbitrary"), vmem_limit_bytes=VMEM_LIMIT_BYTES),
        name="hgrn",
    )(qt, kt, ke, dec, v, gate, *weights)
    return outs[0], outs[1:]


def _moba_kernel(q_ref, k_ref, vt_ref, o_ref, kmean_ref, sel_ref, acc_ref, s_ref):
    BLK, hd, H = MOBA_BLOCK, MOBA_HEAD_DIM, MOBA_HEADS
    TQ = q_ref.shape[0]
    n_blocks = k_ref.shape[0] // BLK
    i = pl.program_id(1)
    own0 = 2 * i

    @pl.when(i == 0)
    def _():
        for j in range(n_blocks):
            kb = k_ref[j * BLK:(j + 1) * BLK, :].astype(F32)
            kmean_ref[j:j + 1, :] = jnp.sum(kb, axis=0, keepdims=True) * (1.0 / BLK)

    blk_id = lax.broadcasted_iota(jnp.int32, (n_blocks, TQ), 0)
    second = lax.broadcasted_iota(jnp.int32, (n_blocks, TQ), 1) >= BLK
    past = blk_id < own0 + second.astype(jnp.int32)
    k_pos = lax.broadcasted_iota(jnp.int32, (BLK, TQ), 0)
    q_lane = lax.broadcasted_iota(jnp.int32, (1, TQ), 1)

    def scores_into(slot, j, h):
        rows = pl.ds(pl.multiple_of(jnp.minimum(j, n_blocks - 1) * BLK, BLK), BLK)
        hs = slice(h * hd, (h + 1) * hd)
        s_ref[slot, h] = _dot_nt(k_ref[rows, hs], q_ref[:, hs])

    SLOT_OWN = (4, 5)
    PAIR_A, PAIR_B = (0, 1), (2, 3)
    gates = [_dot_nt(kmean_ref[:, h * hd:(h + 1) * hd].astype(BF16), q_ref[:, h * hd:(h + 1) * hd])
             for h in range(H)]
    for h in range(H):
        scores_into(SLOT_OWN[0], own0, h)
        scores_into(SLOT_OWN[1], own0 + 1, h)
    for h in range(H):
        scores_into(PAIR_A[0], 0, h)
        scores_into(PAIR_A[1], 1, h)

    m_init = []
    for h in range(H):
        g = jnp.where(past, gates[h], -jnp.inf)
        chosen = jnp.zeros(g.shape, F32)
        for _ in range(MOBA_TOP_K):
            m = jnp.max(g, axis=0, keepdims=True)
            idx = jnp.min(jnp.where(g == m, blk_id, n_blocks), axis=0, keepdims=True)
            hit = blk_id == idx
            chosen = jnp.where(hit, 1.0, chosen)
            g = jnp.where(hit, -jnp.inf, g)
        sel_ref[h] = jnp.where(past, chosen, 0.0)
        picked0 = sel_ref[h, pl.ds(own0, 1), :] > 0.0
        limit0 = jnp.where(q_lane < BLK, q_lane, jnp.where(picked0, BLK, -1))
        limit1 = jnp.where(q_lane < BLK, -1, q_lane - BLK)
        s0 = jnp.where(k_pos <= limit0, s_ref[SLOT_OWN[0], h], NEG_BIG)
        s1 = jnp.where(k_pos <= limit1, s_ref[SLOT_OWN[1], h], NEG_BIG)
        m0 = jnp.maximum(jnp.max(s0, axis=0, keepdims=True), jnp.max(s1, axis=0, keepdims=True))
        acc_ref[h] = (_dot(vt_ref[h, own0], jnp.exp2(s0 - m0).astype(BF16))
                      + _dot(vt_ref[h, own0 + 1], jnp.exp2(s1 - m0).astype(BF16)))
        m_init.append(m0)

    def consume_pair(slots, j, h, m):
        sx, sy = s_ref[slots[0], h], s_ref[slots[1], h]
        px = sel_ref[h, pl.ds(j, 1), :] > 0.0
        py = sel_ref[h, pl.ds(j + 1, 1), :] > 0.0
        m_new = jnp.maximum(m, jnp.maximum(
            jnp.where(px, jnp.max(sx, axis=0, keepdims=True), NEG_BIG),
            jnp.where(py, jnp.max(sy, axis=0, keepdims=True), NEG_BIG)))
        ex = jnp.exp2(sx - jnp.where(px, m_new, -NEG_BIG)).astype(BF16)
        ey = jnp.exp2(sy - jnp.where(py, m_new, -NEG_BIG)).astype(BF16)
        acc_ref[h] = (jnp.exp2(m - m_new) * acc_ref[h]
                      + (_dot(vt_ref[h, j], ex) + _dot(vt_ref[h, j + 1], ey)))
        return m_new

    def body(t, ms):
        ms = list(ms)
        base = 4 * t

        def produce(pair, h):
            j = base + 2 if pair == "B" else base + 4
            slots = PAIR_B if pair == "B" else PAIR_A
            scores_into(slots[0], j, h)
            scores_into(slots[1], j + 1, h)

        def consume(pair, h):
            j = base if pair == "A" else base + 2
            ms[h] = consume_pair(PAIR_A if pair == "A" else PAIR_B, j, h, ms[h])

        for op in "PB0 PB1 CA0 PB2 CA1 PB3 CA2 PA0 CA3 PA1 CB0 PA2 CB1 PA3 CB2 CB3".split():
            (produce if op[0] == "P" else consume)(op[1], int(op[2]))
        return tuple(ms)

    ms = lax.fori_loop(0, i // 2, body, tuple(m_init))

    @pl.when(i % 2 == 1)
    def _():
        for h in range(H):
            consume_pair(PAIR_A, 2 * (i - 1), h, ms[h])

    for h in range(H):
        o = acc_ref[h, :hd, :] / acc_ref[h, hd:hd + 1, :]
        o_ref[:, h * hd:(h + 1) * hd] = o.T.astype(o_ref.dtype)


def _moba(pm, vt, batch, seq):
    n = pm.shape[0]
    BLK, hd, H = MOBA_BLOCK, MOBA_HEAD_DIM, MOBA_HEADS
    TQ = MOBA_QUERY_BLOCKS * BLK
    n_blocks = seq // BLK
    steps = seq // TQ
    vt_rows = vt.shape[-2]
    return pl.pallas_call(
        _moba_kernel,
        grid=(batch, steps),
        in_specs=[
            pl.BlockSpec((TQ, H * hd), lambda b, i: (b * steps + i, 0)),
            pl.BlockSpec((seq, H * hd), lambda b, i: (b, 1)),
            pl.BlockSpec((None, H, n_blocks, vt_rows, BLK), lambda b, i: (b, 0, 0, 0, 0)),
        ],
        out_specs=pl.BlockSpec((TQ, H * hd), lambda b, i: (b * steps + i, 0)),
        out_shape=jax.ShapeDtypeStruct((n, MOBA_WIDTH), BF16),
        scratch_shapes=[
            pltpu.VMEM((n_blocks, H * hd), F32),
            pltpu.VMEM((H, n_blocks, TQ), F32),
            pltpu.VMEM((H, vt_rows, TQ), F32),
            pltpu.VMEM((6, H, BLK, TQ), F32),
        ],
        compiler_params=pltpu.CompilerParams(
            dimension_semantics=("arbitrary", "arbitrary"),
            vmem_limit_bytes=VMEM_LIMIT_BYTES),
        name="moba",
    )(pm, pm, vt)


def _mix_xattn_kernel(x_ref, oh_ref, om_ref, wo_ref, g_ref, wq_ref, mem_ref, gm_ref, wk_ref, wv_ref,
                      wxo_ref, y_ref, k_ref, v_ref):
    @pl.when(pl.program_id(1) == 0)
    def _():
        hm = _rms(mem_ref[...], gm_ref[...]).astype(BF16)
        k_ref[...] = _dot(hm, wk_ref[...]).astype(BF16)
        v_ref[...] = _dot(hm, wv_ref[...]).astype(BF16)

    x1 = (x_ref[...] + _dot(oh_ref[...], wo_ref[:HGRN_WIDTH, :])
          + _dot(om_ref[...], wo_ref[HGRN_WIDTH:, :]))
    h = _rms(x1, g_ref[...]).astype(BF16)
    q = (_dot(h, wq_ref[...]) * (XATTN_HEAD_DIM ** -0.5 * LOG2E)).astype(BF16)
    outs = []
    for hd in range(XATTN_HEADS):
        hs = slice(hd * XATTN_HEAD_DIM, (hd + 1) * XATTN_HEAD_DIM)
        s = _dot_nt(q[:, hs], k_ref[:, hs])
        p = jnp.exp2(s - jnp.max(s, axis=-1, keepdims=True))
        l = jnp.sum(p, axis=-1, keepdims=True)
        outs.append((_dot(p.astype(BF16), v_ref[:, hs]) / l).astype(BF16))
    o = jnp.concatenate(outs, axis=-1)
    y_ref[...] = x1 + _dot(o, wxo_ref[...])


def _mix_xattn(xf, oh, om, wo, g, wq, memf, gm, wk, wv, wxo, batch, seq):
    n = xf.shape[0]
    tm = TOKEN_TILE
    steps = seq // tm
    tok = lambda w: pl.BlockSpec((tm, w), lambda b, i: (b * steps + i, 0))
    return pl.pallas_call(
        _mix_xattn_kernel,
        grid=(batch, steps),
        in_specs=[
            tok(D_MODEL), tok(HGRN_WIDTH), tok(MOBA_WIDTH),
            _const_spec((D_MODEL, D_MODEL)),
            _const_spec((1, D_MODEL)),
            _const_spec((D_MODEL, D_MODEL)),
            pl.BlockSpec((MEM_LEN, D_MODEL), lambda b, i: (b, 0)),
            _const_spec((1, D_MODEL)),
            _const_spec((D_MODEL, D_MODEL)),
            _const_spec((D_MODEL, D_MODEL)),
            _const_spec((D_MODEL, D_MODEL)),
        ],
        out_specs=tok(D_MODEL),
        out_shape=jax.ShapeDtypeStruct((n, D_MODEL), F32),
        scratch_shapes=[pltpu.VMEM((MEM_LEN, D_MODEL), BF16), pltpu.VMEM((MEM_LEN, D_MODEL), BF16)],
        compiler_params=pltpu.CompilerParams(
            dimension_semantics=("arbitrary", "arbitrary"), vmem_limit_bytes=VMEM_LIMIT_BYTES),
        name="mix_xattn",
    )(xf, oh, om, wo, g, wq, memf, gm, wk, wv, wxo)


def _mlp_kernel(x_ref, g_ref, w1_ref, w2_ref, gf_ref, y_ref, *, final_norm):
    x = x_ref[...]
    h = _rms(x, g_ref[...]).astype(BF16)
    ff_chunk = D_MODEL
    acc = x
    for c in range(D_FF // ff_chunk):
        cs = slice(c * ff_chunk, (c + 1) * ff_chunk)
        z = jnp.maximum(_dot(h, w1_ref[:, cs]), 0.0)
        acc = acc + _dot((z * z).astype(BF16), w2_ref[cs, :])
    y_ref[...] = _rms(acc, gf_ref[...]) if final_norm else acc


def _mlp(xf, g, w1, w2, gf, final_norm):
    n = xf.shape[0]
    tm = TOKEN_TILE
    return pl.pallas_call(
        functools.partial(_mlp_kernel, final_norm=final_norm),
        grid=(n // tm,),
        in_specs=[
            pl.BlockSpec((tm, D_MODEL), lambda i: (i, 0)),
            _const_spec((1, D_MODEL)),
            _const_spec((D_MODEL, D_FF)),
            _const_spec((D_FF, D_MODEL)),
            _const_spec((1, D_MODEL)),
        ],
        out_specs=pl.BlockSpec((tm, D_MODEL), lambda i: (i, 0)),
        out_shape=jax.ShapeDtypeStruct((n, D_MODEL), F32),
        compiler_params=pltpu.CompilerParams(
            dimension_semantics=("arbitrary",), vmem_limit_bytes=VMEM_LIMIT_BYTES),
        name="mlp",
    )(xf, g, w1, w2, gf)


def _rotary_tables(seq):
    half = ROT_DIM // 2
    inv_freq = np.float32(ROPE_THETA) ** (-np.arange(half, dtype=np.float32) * np.float32(2.0 / ROT_DIM))
    ang = (np.arange(seq, dtype=np.float32)[:, None] * inv_freq[None, :]).astype(np.float64)
    cos, sin = np.cos(ang), np.sin(ang)
    rest = MOBA_HEAD_DIM - ROT_DIM
    cos_t = np.concatenate([cos, cos, np.ones((seq, rest))], axis=-1)
    sa_t = np.concatenate([-sin, np.zeros((seq, MOBA_HEAD_DIM - half))], axis=-1)
    sb_t = np.concatenate([np.zeros((seq, half)), sin, np.zeros((seq, rest))], axis=-1)
    return tuple(jnp.asarray(t, F32) for t in (cos_t, sa_t, sb_t))


def _cumsum_matrix():
    T, C = HGRN_TILE, HGRN_CHUNK
    r = np.arange(T)
    same = (r[:, None] // C) == (r[None, :] // C)
    return jnp.asarray(same & (r[None, :] <= r[:, None]), BF16)


def kernel(x, mem, norm_mix, w_in, lb_logits, hgrn_norm, w_out, norm_xattn, norm_mem,
           w_xq, w_xk, w_xv, w_xo, norm_mlp, w_ff1, w_ff2, norm_final):
    batch, seq, _ = x.shape
    n = batch * seq
    xf = x.reshape(n, D_MODEL)
    lb_table = jnp.cumsum(jax.nn.softmax(lb_logits.astype(F32), axis=0), axis=0)
    cos_t, sa_t, sb_t = _rotary_tables(seq)
    cum = _cumsum_matrix()
    row = lambda v: v.reshape(1, -1)
    depth = norm_mix.shape[0]
    for l in range(depth):
        qt, kt, ke, dec, hv, gate, pm, vt = _inproj(
            xf, row(norm_mix[l]), w_in[l].astype(BF16), cos_t, sa_t, sb_t,
            row(lb_table[l]), row(hgrn_norm[l]), cum, batch, seq)
        o_hgrn, (wo, wq, wk, wv, wxo, w1, w2) = _hgrn(
            qt, kt, ke, dec, hv, gate,
            (w_out[l], w_xq[l], w_xk[l], w_xv[l], w_xo[l], w_ff1[l], w_ff2[l]), batch, seq)
        o_moba = _moba(pm, vt, batch, seq)
        x2 = _mix_xattn(xf, o_hgrn, o_moba, wo, row(norm_xattn[l]), wq,
                        mem.reshape(-1, D_MODEL), row(norm_mem[l]), wk, wv, wxo, batch, seq)
        xf = _mlp(x2, row(norm_mlp[l]), w1, w2, row(norm_final), final_norm=(l == depth - 1))
    return xf.reshape(batch, seq, D_MODEL)
```

```python
import functools
import math

import jax
import jax.numpy as jnp
import numpy as np
from jax import lax
from jax.experimental import pallas as pl
from jax.experimental.pallas import tpu as pltpu

F32 = jnp.float32
BF16 = jnp.bfloat16

D_MODEL = 1024
MEM_LEN = 256
HGRN_HEADS = 4
HGRN_WIDTH = 512
HGRN_HEAD_DIM = 128
HGRN_CHUNK = 32
MOBA_HEADS = 4
MOBA_WIDTH = 512
MOBA_HEAD_DIM = 128
MOBA_BLOCK = 256
MOBA_TOP_K = 3
MOBA_QUERY_BLOCKS = 2
ROPE_THETA = 500000.0
ROT_DIM = 32
XATTN_HEADS = 4
XATTN_HEAD_DIM = 256
D_FF = 4096
NORM_EPS = 1e-6
IN_COLS = 4 * HGRN_WIDTH + 3 * MOBA_WIDTH

LOG2E = math.log2(math.e)
NEG_BIG = -1e30

VMEM_LIMIT_BYTES = 56 * 1024 * 1024

BF16_SUBLANES = 16
HGRN_TILE = 256
HGRN_SUBTILES = 8
TOKEN_TILE = 1024


def _rms(x, g):
    return x * lax.rsqrt(jnp.mean(x * x, axis=-1, keepdims=True) + NORM_EPS) * g


def _dot(a, b):
    return jnp.dot(a, b, preferred_element_type=F32)


def _dot_nt(a, b):
    return lax.dot_general(a, b, (((1,), (1,)), ((), ())), preferred_element_type=F32)


def _const_spec(shape):
    zeros = (0,) * len(shape)
    return pl.BlockSpec(shape, lambda *_: zeros, pipeline_mode=pl.Buffered(1))


def _inproj_kernel(x_ref, g_ref, w_ref, cos_ref, sa_ref, sb_ref, lb_ref, ng_ref, cum_ref,
                   qt_ref, kt_ref, ke_ref, dec_ref, v_ref, gate_ref, pm_ref, vt_ref, b_ref, k32_ref):
    h = _rms(x_ref[...], g_ref[...]).astype(BF16)
    W, C, T = HGRN_WIDTH, HGRN_CHUNK, HGRN_TILE
    hg_cols = 4 * W
    hd, BLK = MOBA_HEAD_DIM, MOBA_BLOCK
    tm = x_ref.shape[0]

    lb = lb_ref[...]
    hf = _dot(h, w_ref[:, W:2 * W])
    hq = _dot(h, w_ref[:, 0:W])
    hg = _dot(h, w_ref[:, 3 * W:4 * W])
    v_ref[...] = _dot(h, w_ref[:, 2 * W:3 * W]).astype(BF16)
    f = 0.5 * (1.0 + lb) + (0.5 * (1.0 - lb)) * jnp.tanh(0.5 * hf)
    logf = jnp.log2(f)
    l1 = logf.astype(BF16)
    l2 = (logf - l1.astype(F32)).astype(BF16)
    cum = cum_ref[...]
    for t in range(tm // T):
        rows = slice(t * T, (t + 1) * T)
        b_ref[rows, :] = _dot(cum, l1[rows, :]) + _dot(cum, l2[rows, :])
    b = b_ref[...]
    qt_ref[...] = (hq * jnp.exp2(b)).astype(BF16)
    k32 = (1.0 - f) * jnp.exp2(-b)
    kt_ref[...] = k32.astype(BF16)
    k32_ref[...] = k32
    for c in range(tm // C):
        cs = slice(c * C, (c + 1) * C)
        decay = jnp.exp2(b_ref[(c + 1) * C - 1:(c + 1) * C, :])
        dec_ref[c:c + 1, :] = decay
        ke_ref[cs, :] = (k32_ref[cs, :] * decay).astype(BF16)
    gate_ref[...] = (ng_ref[...] * (0.5 * hg * (1.0 + jnp.tanh(0.5 * hg)))).astype(BF16)

    qk = _dot(h, w_ref[:, hg_cols:hg_cols + 2 * MOBA_WIDTH])
    cos, sa, sb = cos_ref[...], sa_ref[...], sb_ref[...]
    q_scale = hd ** -0.5 * LOG2E
    for o in range(0, 2 * MOBA_WIDTH, hd):
        t = qk[:, o:o + hd]
        r = t * cos + pltpu.roll(t, hd - ROT_DIM // 2, 1) * sa + pltpu.roll(t, ROT_DIM // 2, 1) * sb
        if o < MOBA_WIDTH:
            r = r * q_scale
        pm_ref[:, o:o + hd] = r.astype(BF16)
    v = _dot(h, w_ref[:, hg_cols + 2 * MOBA_WIDTH:])
    ones_rows = jnp.ones((vt_ref.shape[2] - hd, BLK), BF16)
    for head in range(MOBA_HEADS):
        for blk in range(v.shape[0] // BLK):
            vt_ref[head, blk, :hd, :] = v[blk * BLK:(blk + 1) * BLK, head * hd:(head + 1) * hd].T.astype(BF16)
            vt_ref[head, blk, hd:, :] = ones_rows


def _inproj(xf, g, w, cos_t, sa_t, sb_t, lb, ng, cum, batch, seq):
    n = xf.shape[0]
    tm = TOKEN_TILE
    tiles_per_seq = seq // tm
    blocks_per_tile = tm // MOBA_BLOCK
    vt_rows = MOBA_HEAD_DIM + BF16_SUBLANES
    rot = lambda: pl.BlockSpec((tm, MOBA_HEAD_DIM), lambda i: (i % tiles_per_seq, 0))
    tok = lambda width: pl.BlockSpec((tm, width), lambda i: (i, 0))
    per_block = lambda rows: pl.BlockSpec(
        (None, MOBA_HEADS, blocks_per_tile, rows, MOBA_BLOCK),
        lambda i: (i // tiles_per_seq, 0, i % tiles_per_seq, 0, 0))
    hgrn_bf16 = jax.ShapeDtypeStruct((n, HGRN_WIDTH), BF16)
    return pl.pallas_call(
        _inproj_kernel,
        grid=(n // tm,),
        in_specs=[
            tok(D_MODEL),
            _const_spec((1, D_MODEL)),
            _const_spec((D_MODEL, IN_COLS)),
            rot(), rot(), rot(),
            _const_spec((1, HGRN_WIDTH)),
            _const_spec((1, HGRN_WIDTH)),
            _const_spec((HGRN_TILE, HGRN_TILE)),
        ],
        out_specs=[
            tok(HGRN_WIDTH), tok(HGRN_WIDTH), tok(HGRN_WIDTH),
            pl.BlockSpec((tm // HGRN_CHUNK, HGRN_WIDTH), lambda i: (i, 0)),
            tok(HGRN_WIDTH), tok(HGRN_WIDTH),
            tok(2 * MOBA_WIDTH),
            per_block(vt_rows),
        ],
        out_shape=[
            hgrn_bf16, hgrn_bf16, hgrn_bf16,
            jax.ShapeDtypeStruct((n // HGRN_CHUNK, HGRN_WIDTH), F32),
            hgrn_bf16, hgrn_bf16,
            jax.ShapeDtypeStruct((n, 2 * MOBA_WIDTH), BF16),
            jax.ShapeDtypeStruct((batch, MOBA_HEADS, seq // MOBA_BLOCK, vt_rows, MOBA_BLOCK), BF16),
        ],
        scratch_shapes=[pltpu.VMEM((tm, HGRN_WIDTH), F32), pltpu.VMEM((tm, HGRN_WIDTH), F32)],
        compiler_params=pltpu.CompilerParams(
            dimension_semantics=("arbitrary",), vmem_limit_bytes=VMEM_LIMIT_BYTES),
        name="inproj",
    )(xf, g, w, cos_t, sa_t, sb_t, lb, ng, cum)


def _hgrn_kernel(qt_ref, kt_ref, ke_ref, dec_ref, v_ref, gate_ref, *rest, n_cast):
    w_refs, o_ref, wb_refs = rest[:n_cast], rest[n_cast], rest[n_cast + 1:2 * n_cast + 1]
    st_ref, keb_ref, oi_ref = rest[2 * n_cast + 1:]
    for w_ref, wb_ref in zip(w_refs, wb_refs):
        wb_ref[...] = w_ref[...].astype(BF16)
    T, C, dh = HGRN_TILE, HGRN_CHUNK, HGRN_HEAD_DIM
    n_chunks = T // C

    @pl.when((pl.program_id(0) == 0) & (pl.program_id(1) == 0))
    def _():
        keb_ref[...] = jnp.zeros_like(keb_ref)

    @pl.when(pl.program_id(1) == 0)
    def _():
        st_ref[...] = jnp.zeros_like(st_ref)

    row = lax.broadcasted_iota(jnp.int32, (T, T), 0)
    col = lax.broadcasted_iota(jnp.int32, (T, T), 1)
    intra_mask = (row // C == col // C) & (col <= row)

    for u in range(HGRN_SUBTILES):
        rows = slice(u * T, (u + 1) * T)
        for h in range(HGRN_HEADS):
            hs = slice(h * dh, (h + 1) * dh)
            for c in range(n_chunks):
                keb_ref[u, h, c * C:(c + 1) * C, c * dh:(c + 1) * dh] = ke_ref[u * T + c * C:u * T + (c + 1) * C, hs]
            vb = v_ref[rows, hs]
            a = jnp.where(intra_mask, _dot_nt(qt_ref[rows, hs], kt_ref[rows, hs]), 0.0)
            upd = _dot(vb.astype(F32).T.astype(BF16), keb_ref[u, h])
            oi_ref[u, h] = _dot(a.astype(BF16), vb)
            st = st_ref[h]
            for c in range(n_chunks):
                cs = slice(c * C, (c + 1) * C)
                oi_ref[u, h, cs, :] += _dot_nt(qt_ref[u * T + c * C:u * T + (c + 1) * C, hs],
                                               st.astype(BF16))
                st = st * dec_ref[u * n_chunks + c:u * n_chunks + c + 1, hs] + upd[:, c * dh:(c + 1) * dh]
            st_ref[h] = st
            o_h = oi_ref[u, h]
            o_h = o_h * lax.rsqrt(jnp.mean(o_h * o_h, axis=-1, keepdims=True) + NORM_EPS)
            o_ref[rows, hs] = (o_h * gate_ref[rows, hs].astype(F32)).astype(o_ref.dtype)


def _hgrn(qt, kt, ke, dec, v, gate, weights, batch, seq):
    n = qt.shape[0]
    T, U = HGRN_TILE, HGRN_SUBTILES
    steps = seq // (T * U)
    n_chunks = T // HGRN_CHUNK
    tok = lambda: pl.BlockSpec((T * U, HGRN_WIDTH), lambda b, i: (b * steps + i, 0))
    row_block = lambda w: pl.BlockSpec((w.shape[0] // (batch * steps), w.shape[1]),
                                       lambda b, i: (b * steps + i, 0))
    outs = pl.pallas_call(
        functools.partial(_hgrn_kernel, n_cast=len(weights)),
        grid=(batch, steps),
        in_specs=[
            tok(), tok(), tok(),
            pl.BlockSpec((U * n_chunks, HGRN_WIDTH), lambda b, i: (b * steps + i, 0)),
            tok(), tok(),
        ] + [row_block(w) for w in weights],
        out_specs=[tok()] + [row_block(w) for w in weights],
        out_shape=[jax.ShapeDtypeStruct((n, HGRN_WIDTH), BF16)]
        + [jax.ShapeDtypeStruct(w.shape, BF16) for w in weights],
        scratch_shapes=[
            pltpu.VMEM((HGRN_HEADS, HGRN_HEAD_DIM, HGRN_HEAD_DIM), F32),
            pltpu.VMEM((U, HGRN_HEADS, T, n_chunks * HGRN_HEAD_DIM), BF16),
            pltpu.VMEM((U, HGRN_HEADS, T, HGRN_HEAD_DIM), F32),
        ],
        compiler_params=pltpu.CompilerParams(
            dimension_semantics=("arbitrary", "arbitrary"), vmem_limit_bytes=VMEM_LIMIT_BYTES),
        name="hgrn",
    )(qt, kt, ke, dec, v, gate, *weights)
    return outs[0], outs[1:]


def _moba_kernel(q_ref, k_ref, vt_ref, o_ref, kmean_ref, sel_ref, acc_ref, s_ref):
    BLK, hd, H = MOBA_BLOCK, MOBA_HEAD_DIM, MOBA_HEADS
    TQ = q_ref.shape[0]
    n_blocks = k_ref.shape[0] // BLK
    i = pl.program_id(1)
    own0 = 2 * i

    @pl.when(i == 0)
    def _():
        for j in range(n_blocks):
            kb = k_ref[j * BLK:(j + 1) * BLK, :].astype(F32)
            kmean_ref[j:j + 1, :] = jnp.sum(kb, axis=0, keepdims=True) * (1.0 / BLK)

    blk_id = lax.broadcasted_iota(jnp.int32, (n_blocks, TQ), 0)
    second = lax.broadcasted_iota(jnp.int32, (n_blocks, TQ), 1) >= BLK
    past = blk_id < own0 + second.astype(jnp.int32)
    k_pos = lax.broadcasted_iota(jnp.int32, (BLK, TQ), 0)
    q_lane = lax.broadcasted_iota(jnp.int32, (1, TQ), 1)

    def scores_into(slot, j, h):
        rows = pl.ds(pl.multiple_of(jnp.minimum(j, n_blocks - 1) * BLK, BLK), BLK)
        hs = slice(h * hd, (h + 1) * hd)
        s_ref[slot, h] = _dot_nt(k_ref[rows, hs], q_ref[:, hs])

    SLOT_OWN = (4, 5)
    PAIR_A, PAIR_B = (0, 1), (2, 3)
    gates = [_dot_nt(kmean_ref[:, h * hd:(h + 1) * hd].astype(BF16), q_ref[:, h * hd:(h + 1) * hd])
             for h in range(H)]
    causal = (lax.broadcasted_iota(jnp.int32, (BLK, BLK), 0)
              <= lax.broadcasted_iota(jnp.int32, (BLK, BLK), 1))
    own1_rows = pl.ds(pl.multiple_of((own0 + 1) * BLK, BLK), BLK)
    for h in range(H):
        scores_into(SLOT_OWN[0], own0, h)
        s_ref[SLOT_OWN[1], h, :, :BLK] = _dot_nt(k_ref[own1_rows, h * hd:(h + 1) * hd],
                                                 q_ref[BLK:, h * hd:(h + 1) * hd])
    for h in range(H):
        scores_into(PAIR_A[0], 0, h)
        scores_into(PAIR_A[1], 1, h)

    m_init = []
    for h in range(H):
        g = jnp.where(past, gates[h], -jnp.inf)
        chosen = jnp.zeros(g.shape, F32)
        for _ in range(MOBA_TOP_K):
            m = jnp.max(g, axis=0, keepdims=True)
            idx = jnp.min(jnp.where(g == m, blk_id, n_blocks), axis=0, keepdims=True)
            hit = blk_id == idx
            chosen = jnp.where(hit, 1.0, chosen)
            g = jnp.where(hit, -jnp.inf, g)
        sel_ref[h] = jnp.where(past, chosen, 0.0)
        picked0 = sel_ref[h, pl.ds(own0, 1), :] > 0.0
        limit0 = jnp.where(q_lane < BLK, q_lane, jnp.where(picked0, BLK, -1))
        s0 = jnp.where(k_pos <= limit0, s_ref[SLOT_OWN[0], h], NEG_BIG)
        s1 = jnp.where(causal, s_ref[SLOT_OWN[1], h, :, :BLK], NEG_BIG)
        m_a = jnp.max(s0[:, :BLK], axis=0, keepdims=True)
        m_b = jnp.maximum(jnp.max(s0[:, BLK:], axis=0, keepdims=True), jnp.max(s1, axis=0, keepdims=True))
        acc_ref[h, :, :BLK] = _dot(vt_ref[h, own0], jnp.exp2(s0[:, :BLK] - m_a).astype(BF16))
        acc_ref[h, :, BLK:] = (_dot(vt_ref[h, own0], jnp.exp2(s0[:, BLK:] - m_b).astype(BF16))
                               + _dot(vt_ref[h, own0 + 1], jnp.exp2(s1 - m_b).astype(BF16)))
        m_init.append(jnp.concatenate([m_a, m_b], axis=1))

    def consume_pair(slots, j, h, m):
        sx, sy = s_ref[slots[0], h], s_ref[slots[1], h]
        px = sel_ref[h, pl.ds(j, 1), :] > 0.0
        py = sel_ref[h, pl.ds(j + 1, 1), :] > 0.0
        m_new = jnp.maximum(m, jnp.maximum(
            jnp.where(px, jnp.max(sx, axis=0, keepdims=True), NEG_BIG),
            jnp.where(py, jnp.max(sy, axis=0, keepdims=True), NEG_BIG)))
        ex = jnp.exp2(sx - jnp.where(px, m_new, -NEG_BIG)).astype(BF16)
        ey = jnp.exp2(sy - jnp.where(py, m_new, -NEG_BIG)).astype(BF16)
        acc_ref[h] = (jnp.exp2(m - m_new) * acc_ref[h]
                      + (_dot(vt_ref[h, j], ex) + _dot(vt_ref[h, j + 1], ey)))
        return m_new

    def body(t, ms):
        ms = list(ms)
        base = 4 * t

        def produce(pair, h):
            j = base + 2 if pair == "B" else base + 4
            slots = PAIR_B if pair == "B" else PAIR_A
            scores_into(slots[0], j, h)
            scores_into(slots[1], j + 1, h)

        def consume(pair, h):
            j = base if pair == "A" else base + 2
            ms[h] = consume_pair(PAIR_A if pair == "A" else PAIR_B, j, h, ms[h])

        for op in "PB0 PB1 CA0 PB2 CA1 PB3 CA2 PA0 CA3 PA1 CB0 PA2 CB1 CB2 PA3 CB3".split():
            (produce if op[0] == "P" else consume)(op[1], int(op[2]))
        return tuple(ms)

    ms = lax.fori_loop(0, i // 2, body, tuple(m_init))

    @pl.when(i % 2 == 1)
    def _():
        for h in range(H):
            consume_pair(PAIR_A, 2 * (i - 1), h, ms[h])

    for h in range(H):
        o = acc_ref[h, :hd, :] / acc_ref[h, hd:hd + 1, :]
        o_ref[:, h * hd:(h + 1) * hd] = o.T.astype(o_ref.dtype)


def _moba(pm, vt, batch, seq):
    n = pm.shape[0]
    BLK, hd, H = MOBA_BLOCK, MOBA_HEAD_DIM, MOBA_HEADS
    TQ = MOBA_QUERY_BLOCKS * BLK
    n_blocks = seq // BLK
    steps = seq // TQ
    vt_rows = vt.shape[-2]
    return pl.pallas_call(
        _moba_kernel,
        grid=(batch, steps),
        in_specs=[
            pl.BlockSpec((TQ, H * hd), lambda b, i: (b * steps + i, 0)),
            pl.BlockSpec((seq, H * hd), lambda b, i: (b, 1)),
            pl.BlockSpec((None, H, n_blocks, vt_rows, BLK), lambda b, i: (b, 0, 0, 0, 0)),
        ],
        out_specs=pl.BlockSpec((TQ, H * hd), lambda b, i: (b * steps + i, 0)),
        out_shape=jax.ShapeDtypeStruct((n, MOBA_WIDTH), BF16),
        scratch_shapes=[
            pltpu.VMEM((n_blocks, H * hd), F32),
            pltpu.VMEM((H, n_blocks, TQ), F32),
            pltpu.VMEM((H, vt_rows, TQ), F32),
            pltpu.VMEM((6, H, BLK, TQ), F32),
        ],
        compiler_params=pltpu.CompilerParams(
            dimension_semantics=("arbitrary", "arbitrary"),
            vmem_limit_bytes=VMEM_LIMIT_BYTES),
        name="moba",
    )(pm, pm, vt)


def _mix_xattn_kernel(x_ref, oh_ref, om_ref, wo_ref, g_ref, wq_ref, mem_ref, gm_ref, wk_ref, wv_ref,
                      wxo_ref, y_ref, k_ref, v_ref):
    @pl.when(pl.program_id(1) == 0)
    def _():
        hm = _rms(mem_ref[...], gm_ref[...]).astype(BF16)
        k_ref[...] = _dot(hm, wk_ref[...]).astype(BF16)
        v_ref[...] = _dot(hm, wv_ref[...]).astype(BF16)

    x1 = (x_ref[...] + _dot(oh_ref[...], wo_ref[:HGRN_WIDTH, :])
          + _dot(om_ref[...], wo_ref[HGRN_WIDTH:, :]))
    h = _rms(x1, g_ref[...]).astype(BF16)
    q = (_dot(h, wq_ref[...]) * (XATTN_HEAD_DIM ** -0.5 * LOG2E)).astype(BF16)
    outs = []
    for hd in range(XATTN_HEADS):
        hs = slice(hd * XATTN_HEAD_DIM, (hd + 1) * XATTN_HEAD_DIM)
        s = _dot_nt(q[:, hs], k_ref[:, hs])
        p = jnp.exp2(s - jnp.max(s, axis=-1, keepdims=True))
        l = jnp.sum(p, axis=-1, keepdims=True)
        outs.append((_dot(p.astype(BF16), v_ref[:, hs]) / l).astype(BF16))
    o = jnp.concatenate(outs, axis=-1)
    y_ref[...] = x1 + _dot(o, wxo_ref[...])


def _mix_xattn(xf, oh, om, wo, g, wq, memf, gm, wk, wv, wxo, batch, seq):
    n = xf.shape[0]
    tm = TOKEN_TILE
    steps = seq // tm
    tok = lambda w: pl.BlockSpec((tm, w), lambda b, i: (b * steps + i, 0))
    return pl.pallas_call(
        _mix_xattn_kernel,
        grid=(batch, steps),
        in_specs=[
            tok(D_MODEL), tok(HGRN_WIDTH), tok(MOBA_WIDTH),
            _const_spec((D_MODEL, D_MODEL)),
            _const_spec((1, D_MODEL)),
            _const_spec((D_MODEL, D_MODEL)),
            pl.BlockSpec((MEM_LEN, D_MODEL), lambda b, i: (b, 0)),
            _const_spec((1, D_MODEL)),
            _const_spec((D_MODEL, D_MODEL)),
            _const_spec((D_MODEL, D_MODEL)),
            _const_spec((D_MODEL, D_MODEL)),
        ],
        out_specs=tok(D_MODEL),
        out_shape=jax.ShapeDtypeStruct((n, D_MODEL), F32),
        scratch_shapes=[pltpu.VMEM((MEM_LEN, D_MODEL), BF16), pltpu.VMEM((MEM_LEN, D_MODEL), BF16)],
        compiler_params=pltpu.CompilerParams(
            dimension_semantics=("arbitrary", "arbitrary"), vmem_limit_bytes=VMEM_LIMIT_BYTES),
        name="mix_xattn",
    )(xf, oh, om, wo, g, wq, memf, gm, wk, wv, wxo)


def _mlp_kernel(x_ref, g_ref, w1_ref, w2_ref, gf_ref, y_ref, *, final_norm):
    x = x_ref[...]
    h = _rms(x, g_ref[...]).astype(BF16)
    ff_chunk = D_MODEL
    acc = x
    for c in range(D_FF // ff_chunk):
        cs = slice(c * ff_chunk, (c + 1) * ff_chunk)
        z = jnp.maximum(_dot(h, w1_ref[:, cs]), 0.0)
        acc = acc + _dot((z * z).astype(BF16), w2_ref[cs, :])
    y_ref[...] = _rms(acc, gf_ref[...]) if final_norm else acc


def _mlp(xf, g, w1, w2, gf, final_norm):
    n = xf.shape[0]
    tm = TOKEN_TILE
    return pl.pallas_call(
        functools.partial(_mlp_kernel, final_norm=final_norm),
        grid=(n // tm,),
        in_specs=[
            pl.BlockSpec((tm, D_MODEL), lambda i: (i, 0)),
            _const_spec((1, D_MODEL)),
            _const_spec((D_MODEL, D_FF)),
            _const_spec((D_FF, D_MODEL)),
            _const_spec((1, D_MODEL)),
        ],
        out_specs=pl.BlockSpec((tm, D_MODEL), lambda i: (i, 0)),
        out_shape=jax.ShapeDtypeStruct((n, D_MODEL), F32),
        compiler_params=pltpu.CompilerParams(
            dimension_semantics=("arbitrary",), vmem_limit_bytes=VMEM_LIMIT_BYTES),
        name="mlp",
    )(xf, g, w1, w2, gf)


def _rotary_tables(seq):
    half = ROT_DIM // 2
    inv_freq = np.float32(ROPE_THETA) ** (-np.arange(half, dtype=np.float32) * np.float32(2.0 / ROT_DIM))
    ang = (np.arange(seq, dtype=np.float32)[:, None] * inv_freq[None, :]).astype(np.float64)
    cos, sin = np.cos(ang), np.sin(ang)
    rest = MOBA_HEAD_DIM - ROT_DIM
    cos_t = np.concatenate([cos, cos, np.ones((seq, rest))], axis=-1)
    sa_t = np.concatenate([-sin, np.zeros((seq, MOBA_HEAD_DIM - half))], axis=-1)
    sb_t = np.concatenate([np.zeros((seq, half)), sin, np.zeros((seq, rest))], axis=-1)
    return tuple(jnp.asarray(t, F32) for t in (cos_t, sa_t, sb_t))


def _cumsum_matrix():
    T, C = HGRN_TILE, HGRN_CHUNK
    r = np.arange(T)
    same = (r[:, None] // C) == (r[None, :] // C)
    return jnp.asarray(same & (r[None, :] <= r[:, None]), BF16)


def kernel(x, mem, norm_mix, w_in, lb_logits, hgrn_norm, w_out, norm_xattn, norm_mem,
           w_xq, w_xk, w_xv, w_xo, norm_mlp, w_ff1, w_ff2, norm_final):
    batch, seq, _ = x.shape
    n = batch * seq
    xf = x.reshape(n, D_MODEL)
    lb_table = jnp.cumsum(jax.nn.softmax(lb_logits.astype(F32), axis=0), axis=0)
    cos_t, sa_t, sb_t = _rotary_tables(seq)
    cum = _cumsum_matrix()
    row = lambda v: v.reshape(1, -1)
    depth = norm_mix.shape[0]
    for l in range(depth):
        qt, kt, ke, dec, hv, gate, pm, vt = _inproj(
            xf, row(norm_mix[l]), w_in[l].astype(BF16), cos_t, sa_t, sb_t,
            row(lb_table[l]), row(hgrn_norm[l]), cum, batch, seq)
        o_hgrn, (wo, wq, wk, wv, wxo, w1, w2) = _hgrn(
            qt, kt, ke, dec, hv, gate,
            (w_out[l], w_xq[l], w_xk[l], w_xv[l], w_xo[l], w_ff1[l], w_ff2[l]), batch, seq)
        o_moba = _moba(pm, vt, batch, seq)
        x2 = _mix_xattn(xf, o_hgrn, o_moba, wo, row(norm_xattn[l]), wq,
                        mem.reshape(-1, D_MODEL), row(norm_mem[l]), wk, wv, wxo, batch, seq)
        xf = _mlp(x2, row(norm_mlp[l]), w1, w2, row(norm_final), final_norm=(l == depth - 1))
    return xf.reshape(batch, seq, D_MODEL)
```

```python
import functools
import math

import jax
import jax.numpy as jnp
import numpy as np
from jax import lax
from jax.experimental import pallas as pl
from jax.experimental.pallas import tpu as pltpu

F32 = jnp.float32
BF16 = jnp.bfloat16

D_MODEL = 1024
MEM_LEN = 256
HGRN_HEADS = 4
HGRN_WIDTH = 512
HGRN_HEAD_DIM = 128
HGRN_CHUNK = 32
MOBA_HEADS = 4
MOBA_WIDTH = 512
MOBA_HEAD_DIM = 128
MOBA_BLOCK = 256
MOBA_TOP_K = 3
MOBA_QUERY_BLOCKS = 2
ROPE_THETA = 500000.0
ROT_DIM = 32
XATTN_HEADS = 4
XATTN_HEAD_DIM = 256
D_FF = 4096
NORM_EPS = 1e-6
IN_COLS = 4 * HGRN_WIDTH + 3 * MOBA_WIDTH

LOG2E = math.log2(math.e)
NEG_BIG = -1e30

VMEM_LIMIT_BYTES = 56 * 1024 * 1024

BF16_SUBLANES = 16
HGRN_TILE = 256
HGRN_SUBTILES = 4
TOKEN_TILE = 1024


def _rms(x, g):
    return x * lax.rsqrt(jnp.mean(x * x, axis=-1, keepdims=True) + NORM_EPS) * g


def _dot(a, b):
    return jnp.dot(a, b, preferred_element_type=F32)


def _dot_nt(a, b):
    return lax.dot_general(a, b, (((1,), (1,)), ((), ())), preferred_element_type=F32)


def _const_spec(shape):
    zeros = (0,) * len(shape)
    return pl.BlockSpec(shape, lambda *_: zeros, pipeline_mode=pl.Buffered(1))


def _inproj_kernel(x_ref, g_ref, w_ref, cos_ref, sa_ref, sb_ref, lb_ref, ng_ref, cum_ref,
                   qt_ref, kt_ref, ke_ref, dec_ref, v_ref, gate_ref, pm_ref, vt_ref, b_ref, k32_ref):
    h = _rms(x_ref[...], g_ref[...]).astype(BF16)
    W, C, T = HGRN_WIDTH, HGRN_CHUNK, HGRN_TILE
    hg_cols = 4 * W
    hd, BLK = MOBA_HEAD_DIM, MOBA_BLOCK
    tm = x_ref.shape[0]

    lb = lb_ref[...]
    hf = _dot(h, w_ref[:, W:2 * W])
    hq = _dot(h, w_ref[:, 0:W])
    hg = _dot(h, w_ref[:, 3 * W:4 * W])
    v_ref[...] = _dot(h, w_ref[:, 2 * W:3 * W]).astype(BF16)
    f = 0.5 * (1.0 + lb) + (0.5 * (1.0 - lb)) * jnp.tanh(0.5 * hf)
    logf = jnp.log2(f)
    l1 = logf.astype(BF16)
    l2 = (logf - l1.astype(F32)).astype(BF16)
    cum = cum_ref[...]
    for t in range(tm // T):
        rows = slice(t * T, (t + 1) * T)
        b_ref[rows, :] = _dot(cum, l1[rows, :]) + _dot(cum, l2[rows, :])
    b = b_ref[...]
    qt_ref[...] = (hq * jnp.exp2(b)).astype(BF16)
    k32 = (1.0 - f) * jnp.exp2(-b)
    kt_ref[...] = k32.astype(BF16)
    k32_ref[...] = k32
    for c in range(tm // C):
        cs = slice(c * C, (c + 1) * C)
        decay = jnp.exp2(b_ref[(c + 1) * C - 1:(c + 1) * C, :])
        dec_ref[c:c + 1, :] = decay
        ke_ref[cs, :] = (k32_ref[cs, :] * decay).astype(BF16)
    gate_ref[...] = (ng_ref[...] * (0.5 * hg * (1.0 + jnp.tanh(0.5 * hg)))).astype(BF16)

    qk = _dot(h, w_ref[:, hg_cols:hg_cols + 2 * MOBA_WIDTH])
    cos, sa, sb = cos_ref[...], sa_ref[...], sb_ref[...]
    q_scale = hd ** -0.5 * LOG2E
    for o in range(0, 2 * MOBA_WIDTH, hd):
        t = qk[:, o:o + hd]
        r = t * cos + pltpu.roll(t, hd - ROT_DIM // 2, 1) * sa + pltpu.roll(t, ROT_DIM // 2, 1) * sb
        if o < MOBA_WIDTH:
            r = r * q_scale
        pm_ref[:, o:o + hd] = r.astype(BF16)
    v = _dot(h, w_ref[:, hg_cols + 2 * MOBA_WIDTH:])
    ones_rows = jnp.ones((vt_ref.shape[2] - hd, BLK), BF16)
    for head in range(MOBA_HEADS):
        for blk in range(v.shape[0] // BLK):
            vt_ref[head, blk, :hd, :] = v[blk * BLK:(blk + 1) * BLK, head * hd:(head + 1) * hd].T.astype(BF16)
            vt_ref[head, blk, hd:, :] = ones_rows


def _inproj(xf, g, w, cos_t, sa_t, sb_t, lb, ng, cum, batch, seq):
    n = xf.shape[0]
    tm = TOKEN_TILE
    tiles_per_seq = seq // tm
    blocks_per_tile = tm // MOBA_BLOCK
    vt_rows = MOBA_HEAD_DIM + BF16_SUBLANES
    rot = lambda: pl.BlockSpec((tm, MOBA_HEAD_DIM), lambda i: (i % tiles_per_seq, 0))
    tok = lambda width: pl.BlockSpec((tm, width), lambda i: (i, 0))
    per_block = lambda rows: pl.BlockSpec(
        (None, MOBA_HEADS, blocks_per_tile, rows, MOBA_BLOCK),
        lambda i: (i // tiles_per_seq, 0, i % tiles_per_seq, 0, 0))
    hgrn_bf16 = jax.ShapeDtypeStruct((n, HGRN_WIDTH), BF16)
    return pl.pallas_call(
        _inproj_kernel,
        grid=(n // tm,),
        in_specs=[
            tok(D_MODEL),
            _const_spec((1, D_MODEL)),
            _const_spec((D_MODEL, IN_COLS)),
            rot(), rot(), rot(),
            _const_spec((1, HGRN_WIDTH)),
            _const_spec((1, HGRN_WIDTH)),
            _const_spec((HGRN_TILE, HGRN_TILE)),
        ],
        out_specs=[
            tok(HGRN_WIDTH), tok(HGRN_WIDTH), tok(HGRN_WIDTH),
            pl.BlockSpec((tm // HGRN_CHUNK, HGRN_WIDTH), lambda i: (i, 0)),
            tok(HGRN_WIDTH), tok(HGRN_WIDTH),
            tok(2 * MOBA_WIDTH),
            per_block(vt_rows),
        ],
        out_shape=[
            hgrn_bf16, hgrn_bf16, hgrn_bf16,
            jax.ShapeDtypeStruct((n // HGRN_CHUNK, HGRN_WIDTH), F32),
            hgrn_bf16, hgrn_bf16,
            jax.ShapeDtypeStruct((n, 2 * MOBA_WIDTH), BF16),
            jax.ShapeDtypeStruct((batch, MOBA_HEADS, seq // MOBA_BLOCK, vt_rows, MOBA_BLOCK), BF16),
        ],
        scratch_shapes=[pltpu.VMEM((tm, HGRN_WIDTH), F32), pltpu.VMEM((tm, HGRN_WIDTH), F32)],
        compiler_params=pltpu.CompilerParams(
            dimension_semantics=("arbitrary",), vmem_limit_bytes=VMEM_LIMIT_BYTES),
        name="inproj",
    )(xf, g, w, cos_t, sa_t, sb_t, lb, ng, cum)


def _hgrn_kernel(qt_ref, kt_ref, ke_ref, dec_ref, v_ref, gate_ref, *rest, n_cast):
    w_refs, o_ref, wb_refs = rest[:n_cast], rest[n_cast], rest[n_cast + 1:2 * n_cast + 1]
    st_ref, keb_ref, oi_ref = rest[2 * n_cast + 1:]
    for w_ref, wb_ref in zip(w_refs, wb_refs):
        wb_ref[...] = w_ref[...].astype(BF16)
    T, C, dh = HGRN_TILE, HGRN_CHUNK, HGRN_HEAD_DIM
    n_chunks = T // C

    @pl.when((pl.program_id(0) == 0) & (pl.program_id(1) == 0))
    def _():
        keb_ref[...] = jnp.zeros_like(keb_ref)

    @pl.when(pl.program_id(1) == 0)
    def _():
        st_ref[...] = jnp.zeros_like(st_ref)

    row = lax.broadcasted_iota(jnp.int32, (T, T), 0)
    col = lax.broadcasted_iota(jnp.int32, (T, T), 1)
    intra_mask = (row // C == col // C) & (col <= row)

    for u in range(HGRN_SUBTILES):
        rows = slice(u * T, (u + 1) * T)
        for h in range(HGRN_HEADS):
            hs = slice(h * dh, (h + 1) * dh)
            for c in range(n_chunks):
                keb_ref[u, h, c * C:(c + 1) * C, c * dh:(c + 1) * dh] = ke_ref[u * T + c * C:u * T + (c + 1) * C, hs]
            vb = v_ref[rows, hs]
            a = jnp.where(intra_mask, _dot_nt(qt_ref[rows, hs], kt_ref[rows, hs]), 0.0)
            upd = _dot(vb.astype(F32).T.astype(BF16), keb_ref[u, h])
            oi_ref[u, h] = _dot(a.astype(BF16), vb)
            st = st_ref[h]
            for c in range(n_chunks):
                cs = slice(c * C, (c + 1) * C)
                oi_ref[u, h, cs, :] += _dot_nt(qt_ref[u * T + c * C:u * T + (c + 1) * C, hs],
                                               st.astype(BF16))
                st = st * dec_ref[u * n_chunks + c:u * n_chunks + c + 1, hs] + upd[:, c * dh:(c + 1) * dh]
            st_ref[h] = st
            o_h = oi_ref[u, h]
            o_h = o_h * lax.rsqrt(jnp.mean(o_h * o_h, axis=-1, keepdims=True) + NORM_EPS)
            o_ref[rows, hs] = (o_h * gate_ref[rows, hs].astype(F32)).astype(o_ref.dtype)


def _hgrn(qt, kt, ke, dec, v, gate, weights, batch, seq):
    n = qt.shape[0]
    T, U = HGRN_TILE, HGRN_SUBTILES
    steps = seq // (T * U)
    n_chunks = T // HGRN_CHUNK
    tok = lambda: pl.BlockSpec((T * U, HGRN_WIDTH), lambda b, i: (b * steps + i, 0))
    row_block = lambda w: pl.BlockSpec((w.shape[0] // (batch * steps), w.shape[1]),
                                       lambda b, i: (b * steps + i, 0))
    outs = pl.pallas_call(
        functools.partial(_hgrn_kernel, n_cast=len(weights)),
        grid=(batch, steps),
        in_specs=[
            tok(), tok(), tok(),
            pl.BlockSpec((U * n_chunks, HGRN_WIDTH), lambda b, i: (b * steps + i, 0)),
            tok(), tok(),
        ] + [row_block(w) for w in weights],
        out_specs=[tok()] + [row_block(w) for w in weights],
        out_shape=[jax.ShapeDtypeStruct((n, HGRN_WIDTH), BF16)]
        + [jax.ShapeDtypeStruct(w.shape, BF16) for w in weights],
        scratch_shapes=[
            pltpu.VMEM((HGRN_HEADS, HGRN_HEAD_DIM, HGRN_HEAD_DIM), F32),
            pltpu.VMEM((U, HGRN_HEADS, T, n_chunks * HGRN_HEAD_DIM), BF16),
            pltpu.VMEM((U, HGRN_HEADS, T, HGRN_HEAD_DIM), F32),
        ],
        compiler_params=pltpu.CompilerParams(
            dimension_semantics=("arbitrary", "arbitrary"), vmem_limit_bytes=VMEM_LIMIT_BYTES),
        name="hgrn",
    )(qt, kt, ke, dec, v, gate, *weights)
    return outs[0], outs[1:]


def _moba_kernel(q_ref, k_ref, vt_ref, o_ref, kmean_ref, sel_ref, acc_ref, s_ref):
    BLK, hd, H = MOBA_BLOCK, MOBA_HEAD_DIM, MOBA_HEADS
    TQ = q_ref.shape[0]
    n_blocks = k_ref.shape[0] // BLK
    i = pl.program_id(1)
    own0 = 2 * i

    @pl.when(i == 0)
    def _():
        for j in range(n_blocks):
            kb = k_ref[j * BLK:(j + 1) * BLK, :].astype(F32)
            kmean_ref[j:j + 1, :] = jnp.sum(kb, axis=0, keepdims=True) * (1.0 / BLK)

    blk_id = lax.broadcasted_iota(jnp.int32, (n_blocks, TQ), 0)
    second = lax.broadcasted_iota(jnp.int32, (n_blocks, TQ), 1) >= BLK
    past = blk_id < own0 + second.astype(jnp.int32)
    k_pos = lax.broadcasted_iota(jnp.int32, (BLK, TQ), 0)
    q_lane = lax.broadcasted_iota(jnp.int32, (1, TQ), 1)

    def scores_into(slot, j, h):
        rows = pl.ds(pl.multiple_of(jnp.minimum(j, n_blocks - 1) * BLK, BLK), BLK)
        hs = slice(h * hd, (h + 1) * hd)
        s_ref[slot, h] = _dot_nt(k_ref[rows, hs], q_ref[:, hs])

    PAIR_A, PAIR_B = (0, 1), (2, 3)
    SLOT_OWN = PAIR_B
    gates = [_dot_nt(kmean_ref[:, h * hd:(h + 1) * hd].astype(BF16), q_ref[:, h * hd:(h + 1) * hd])
             for h in range(H)]
    for h in range(H):
        scores_into(SLOT_OWN[0], own0, h)
        scores_into(SLOT_OWN[1], own0 + 1, h)
    for h in range(H):
        scores_into(PAIR_A[0], 0, h)
        scores_into(PAIR_A[1], 1, h)

    m_init = []
    for h in range(H):
        g = jnp.where(past, gates[h], -jnp.inf)
        chosen = jnp.zeros(g.shape, F32)
        for _ in range(MOBA_TOP_K):
            m = jnp.max(g, axis=0, keepdims=True)
            idx = jnp.min(jnp.where(g == m, blk_id, n_blocks), axis=0, keepdims=True)
            hit = blk_id == idx
            chosen = jnp.where(hit, 1.0, chosen)
            g = jnp.where(hit, -jnp.inf, g)
        sel_ref[h] = jnp.where(past, chosen, 0.0)
        picked0 = sel_ref[h, pl.ds(own0, 1), :] > 0.0
        limit0 = jnp.where(q_lane < BLK, q_lane, jnp.where(picked0, BLK, -1))
        limit1 = jnp.where(q_lane < BLK, -1, q_lane - BLK)
        s0 = jnp.where(k_pos <= limit0, s_ref[SLOT_OWN[0], h], NEG_BIG)
        s1 = jnp.where(k_pos <= limit1, s_ref[SLOT_OWN[1], h], NEG_BIG)
        m0 = jnp.maximum(jnp.max(s0, axis=0, keepdims=True), jnp.max(s1, axis=0, keepdims=True))
        acc_ref[h] = (_dot(vt_ref[h, own0], jnp.exp2(s0 - m0).astype(BF16))
                      + _dot(vt_ref[h, own0 + 1], jnp.exp2(s1 - m0).astype(BF16)))
        m_init.append(m0)

    def consume_pair(slots, j, h, m):
        sx, sy = s_ref[slots[0], h], s_ref[slots[1], h]
        px = sel_ref[h, pl.ds(j, 1), :] > 0.0
        py = sel_ref[h, pl.ds(j + 1, 1), :] > 0.0
        m_new = jnp.maximum(m, jnp.maximum(
            jnp.where(px, jnp.max(sx, axis=0, keepdims=True), NEG_BIG),
            jnp.where(py, jnp.max(sy, axis=0, keepdims=True), NEG_BIG)))
        ex = jnp.exp2(sx - jnp.where(px, m_new, -NEG_BIG)).astype(BF16)
        ey = jnp.exp2(sy - jnp.where(py, m_new, -NEG_BIG)).astype(BF16)
        acc_ref[h] = (jnp.exp2(m - m_new) * acc_ref[h]
                      + (_dot(vt_ref[h, j], ex) + _dot(vt_ref[h, j + 1], ey)))
        return m_new

    def four_blocks(base, ms):
        ms = list(ms)

        def produce(pair, h):
            j = base + 2 if pair == "B" else base + 4
            slots = PAIR_B if pair == "B" else PAIR_A
            scores_into(slots[0], j, h)
            scores_into(slots[1], j + 1, h)

        def consume(pair, h):
            j = base if pair == "A" else base + 2
            ms[h] = consume_pair(PAIR_A if pair == "A" else PAIR_B, j, h, ms[h])

        for op in "PB0 PB1 CA0 PB2 CA1 PB3 CA2 PA0 CA3 PA1 CB0 PA2 CB1 PA3 CB2 CB3".split():
            (produce if op[0] == "P" else consume)(op[1], int(op[2]))
        return tuple(ms)

    long_trips = i // 4
    ms = lax.fori_loop(0, long_trips,
                       lambda t, ms: four_blocks(8 * t + 4, four_blocks(8 * t, ms)), tuple(m_init))
    ms = lax.fori_loop(0, (i % 4) // 2, lambda t, ms: four_blocks(8 * long_trips, ms), ms)

    @pl.when(i % 2 == 1)
    def _():
        for h in range(H):
            consume_pair(PAIR_A, 2 * (i - 1), h, ms[h])

    for h in range(H):
        o = acc_ref[h, :hd, :] / acc_ref[h, hd:hd + 1, :]
        o_ref[:, h * hd:(h + 1) * hd] = o.T.astype(o_ref.dtype)


def _moba(pm, vt, batch, seq):
    n = pm.shape[0]
    BLK, hd, H = MOBA_BLOCK, MOBA_HEAD_DIM, MOBA_HEADS
    TQ = MOBA_QUERY_BLOCKS * BLK
    n_blocks = seq // BLK
    steps = seq // TQ
    vt_rows = vt.shape[-2]
    return pl.pallas_call(
        _moba_kernel,
        grid=(batch, steps),
        in_specs=[
            pl.BlockSpec((TQ, H * hd), lambda b, i: (b * steps + i, 0)),
            pl.BlockSpec((seq, H * hd), lambda b, i: (b, 1)),
            pl.BlockSpec((None, H, n_blocks, vt_rows, BLK), lambda b, i: (b, 0, 0, 0, 0)),
        ],
        out_specs=pl.BlockSpec((TQ, H * hd), lambda b, i: (b * steps + i, 0)),
        out_shape=jax.ShapeDtypeStruct((n, MOBA_WIDTH), BF16),
        scratch_shapes=[
            pltpu.VMEM((n_blocks, H * hd), F32),
            pltpu.VMEM((H, n_blocks, TQ), F32),
            pltpu.VMEM((H, vt_rows, TQ), F32),
            pltpu.VMEM((4, H, BLK, TQ), F32),
        ],
        compiler_params=pltpu.CompilerParams(
            dimension_semantics=("arbitrary", "arbitrary"),
            vmem_limit_bytes=VMEM_LIMIT_BYTES),
        name="moba",
    )(pm, pm, vt)


def _mix_xattn_kernel(x_ref, oh_ref, om_ref, wo_ref, g_ref, wq_ref, mem_ref, gm_ref, wk_ref, wv_ref,
                      wxo_ref, y_ref, k_ref, v_ref):
    @pl.when(pl.program_id(1) == 0)
    def _():
        hm = _rms(mem_ref[...], gm_ref[...]).astype(BF16)
        k_ref[...] = _dot(hm, wk_ref[...]).astype(BF16)
        v_ref[...] = _dot(hm, wv_ref[...]).astype(BF16)

    x1 = (x_ref[...] + _dot(oh_ref[...], wo_ref[:HGRN_WIDTH, :])
          + _dot(om_ref[...], wo_ref[HGRN_WIDTH:, :]))
    h = _rms(x1, g_ref[...]).astype(BF16)
    q = (_dot(h, wq_ref[...]) * (XATTN_HEAD_DIM ** -0.5 * LOG2E)).astype(BF16)
    outs = []
    for hd in range(XATTN_HEADS):
        hs = slice(hd * XATTN_HEAD_DIM, (hd + 1) * XATTN_HEAD_DIM)
        s = _dot_nt(q[:, hs], k_ref[:, hs])
        p = jnp.exp2(s - jnp.max(s, axis=-1, keepdims=True))
        l = jnp.sum(p, axis=-1, keepdims=True)
        outs.append((_dot(p.astype(BF16), v_ref[:, hs]) / l).astype(BF16))
    o = jnp.concatenate(outs, axis=-1)
    y_ref[...] = x1 + _dot(o, wxo_ref[...])


def _mix_xattn(xf, oh, om, wo, g, wq, memf, gm, wk, wv, wxo, batch, seq):
    n = xf.shape[0]
    tm = TOKEN_TILE
    steps = seq // tm
    tok = lambda w: pl.BlockSpec((tm, w), lambda b, i: (b * steps + i, 0))
    return pl.pallas_call(
        _mix_xattn_kernel,
        grid=(batch, steps),
        in_specs=[
            tok(D_MODEL), tok(HGRN_WIDTH), tok(MOBA_WIDTH),
            _const_spec((D_MODEL, D_MODEL)),
            _const_spec((1, D_MODEL)),
            _const_spec((D_MODEL, D_MODEL)),
            pl.BlockSpec((MEM_LEN, D_MODEL), lambda b, i: (b, 0)),
            _const_spec((1, D_MODEL)),
            _const_spec((D_MODEL, D_MODEL)),
            _const_spec((D_MODEL, D_MODEL)),
            _const_spec((D_MODEL, D_MODEL)),
        ],
        out_specs=tok(D_MODEL),
        out_shape=jax.ShapeDtypeStruct((n, D_MODEL), F32),
        scratch_shapes=[pltpu.VMEM((MEM_LEN, D_MODEL), BF16), pltpu.VMEM((MEM_LEN, D_MODEL), BF16)],
        compiler_params=pltpu.CompilerParams(
            dimension_semantics=("arbitrary", "arbitrary"), vmem_limit_bytes=VMEM_LIMIT_BYTES),
        name="mix_xattn",
    )(xf, oh, om, wo, g, wq, memf, gm, wk, wv, wxo)


def _mlp_kernel(x_ref, g_ref, w1_ref, w2_ref, gf_ref, y_ref, *, final_norm):
    x = x_ref[...]
    h = _rms(x, g_ref[...]).astype(BF16)
    ff_chunk = D_MODEL
    acc = x
    for c in range(D_FF // ff_chunk):
        cs = slice(c * ff_chunk, (c + 1) * ff_chunk)
        z = jnp.maximum(_dot(h, w1_ref[:, cs]), 0.0)
        acc = acc + _dot((z * z).astype(BF16), w2_ref[cs, :])
    y_ref[...] = _rms(acc, gf_ref[...]) if final_norm else acc


def _mlp(xf, g, w1, w2, gf, final_norm):
    n = xf.shape[0]
    tm = TOKEN_TILE
    return pl.pallas_call(
        functools.partial(_mlp_kernel, final_norm=final_norm),
        grid=(n // tm,),
        in_specs=[
            pl.BlockSpec((tm, D_MODEL), lambda i: (i, 0)),
            _const_spec((1, D_MODEL)),
            _const_spec((D_MODEL, D_FF)),
            _const_spec((D_FF, D_MODEL)),
            _const_spec((1, D_MODEL)),
        ],
        out_specs=pl.BlockSpec((tm, D_MODEL), lambda i: (i, 0)),
        out_shape=jax.ShapeDtypeStruct((n, D_MODEL), F32),
        compiler_params=pltpu.CompilerParams(
            dimension_semantics=("arbitrary",), vmem_limit_bytes=VMEM_LIMIT_BYTES),
        name="mlp",
    )(xf, g, w1, w2, gf)


def _rotary_tables(seq):
    half = ROT_DIM // 2
    inv_freq = np.float32(ROPE_THETA) ** (-np.arange(half, dtype=np.float32) * np.float32(2.0 / ROT_DIM))
    ang = (np.arange(seq, dtype=np.float32)[:, None] * inv_freq[None, :]).astype(np.float64)
    cos, sin = np.cos(ang), np.sin(ang)
    rest = MOBA_HEAD_DIM - ROT_DIM
    cos_t = np.concatenate([cos, cos, np.ones((seq, rest))], axis=-1)
    sa_t = np.concatenate([-sin, np.zeros((seq, MOBA_HEAD_DIM - half))], axis=-1)
    sb_t = np.concatenate([np.zeros((seq, half)), sin, np.zeros((seq, rest))], axis=-1)
    return tuple(jnp.asarray(t, F32) for t in (cos_t, sa_t, sb_t))


def _cumsum_matrix():
    T, C = HGRN_TILE, HGRN_CHUNK
    r = np.arange(T)
    same = (r[:, None] // C) == (r[None, :] // C)
    return jnp.asarray(same & (r[None, :] <= r[:, None]), BF16)


def kernel(x, mem, norm_mix, w_in, lb_logits, hgrn_norm, w_out, norm_xattn, norm_mem,
           w_xq, w_xk, w_xv, w_xo, norm_mlp, w_ff1, w_ff2, norm_final):
    batch, seq, _ = x.shape
    n = batch * seq
    xf = x.reshape(n, D_MODEL)
    lb_table = jnp.cumsum(jax.nn.softmax(lb_logits.astype(F32), axis=0), axis=0)
    cos_t, sa_t, sb_t = _rotary_tables(seq)
    cum = _cumsum_matrix()
    row = lambda v: v.reshape(1, -1)
    depth = norm_mix.shape[0]
    for l in range(depth):
        qt, kt, ke, dec, hv, gate, pm, vt = _inproj(
            xf, row(norm_mix[l]), w_in[l].astype(BF16), cos_t, sa_t, sb_t,
            row(lb_table[l]), row(hgrn_norm[l]), cum, batch, seq)
        o_hgrn, (wo, wq, wk, wv, wxo, w1, w2) = _hgrn(
            qt, kt, ke, dec, hv, gate,
            (w_out[l], w_xq[l], w_xk[l], w_xv[l], w_xo[l], w_ff1[l], w_ff2[l]), batch, seq)
        o_moba = _moba(pm, vt, batch, seq)
        x2 = _mix_xattn(xf, o_hgrn, o_moba, wo, row(norm_xattn[l]), wq,
                        mem.reshape(-1, D_MODEL), row(norm_mem[l]), wk, wv, wxo, batch, seq)
        xf = _mlp(x2, row(norm_mlp[l]), w1, w2, row(norm_final), final_norm=(l == depth - 1))
    return xf.reshape(batch, seq, D_MODEL)
```

```python
import functools
import math

import jax
import jax.numpy as jnp
import numpy as np
from jax import lax
from jax.experimental import pallas as pl
from jax.experimental.pallas import tpu as pltpu

F32 = jnp.float32
BF16 = jnp.bfloat16

D_MODEL = 1024
MEM_LEN = 256
HGRN_HEADS = 4
HGRN_WIDTH = 512
HGRN_HEAD_DIM = 128
HGRN_CHUNK = 32
MOBA_HEADS = 4
MOBA_WIDTH = 512
MOBA_HEAD_DIM = 128
MOBA_BLOCK = 256
MOBA_TOP_K = 3
MOBA_QUERY_BLOCKS = 2
ROPE_THETA = 500000.0
ROT_DIM = 32
XATTN_HEADS = 4
XATTN_HEAD_DIM = 256
D_FF = 4096
NORM_EPS = 1e-6
IN_COLS = 4 * HGRN_WIDTH + 3 * MOBA_WIDTH

LOG2E = math.log2(math.e)
NEG_BIG = -1e30

VMEM_LIMIT_BYTES = 56 * 1024 * 1024

BF16_SUBLANES = 16
HGRN_TILE = 256
HGRN_SUBTILES = 4
TOKEN_TILE = 1024


def _rms(x, g):
    return x * lax.rsqrt(jnp.mean(x * x, axis=-1, keepdims=True) + NORM_EPS) * g


def _dot(a, b):
    return jnp.dot(a, b, preferred_element_type=F32)


def _dot_nt(a, b):
    return lax.dot_general(a, b, (((1,), (1,)), ((), ())), preferred_element_type=F32)


def _const_spec(shape):
    zeros = (0,) * len(shape)
    return pl.BlockSpec(shape, lambda *_: zeros, pipeline_mode=pl.Buffered(1))


def _inproj_kernel(x_ref, g_ref, w_ref, cos_ref, sa_ref, sb_ref, lb_ref, ng_ref, cum_ref,
                   qt_ref, kt_ref, ke_ref, dec_ref, v_ref, gate_ref, pm_ref, vt_ref, b_ref, k32_ref):
    h = _rms(x_ref[...], g_ref[...]).astype(BF16)
    W, C, T = HGRN_WIDTH, HGRN_CHUNK, HGRN_TILE
    hg_cols = 4 * W
    hd, BLK = MOBA_HEAD_DIM, MOBA_BLOCK
    tm = x_ref.shape[0]

    lb = lb_ref[...]
    hf = _dot(h, w_ref[:, W:2 * W])
    hq = _dot(h, w_ref[:, 0:W])
    hg = _dot(h, w_ref[:, 3 * W:4 * W])
    v_ref[...] = _dot(h, w_ref[:, 2 * W:3 * W]).astype(BF16)
    f = 0.5 * (1.0 + lb) + (0.5 * (1.0 - lb)) * jnp.tanh(0.5 * hf)
    logf = jnp.log2(f)
    l1 = logf.astype(BF16)
    l2 = (logf - l1.astype(F32)).astype(BF16)
    cum = cum_ref[...]
    for t in range(tm // T):
        rows = slice(t * T, (t + 1) * T)
        b_ref[rows, :] = _dot(cum, l1[rows, :]) + _dot(cum, l2[rows, :])
    b = b_ref[...]
    qt_ref[...] = (hq * jnp.exp2(b)).astype(BF16)
    k32 = (1.0 - f) * jnp.exp2(-b)
    kt_ref[...] = k32.astype(BF16)
    k32_ref[...] = k32
    for c in range(tm // C):
        cs = slice(c * C, (c + 1) * C)
        decay = jnp.exp2(b_ref[(c + 1) * C - 1:(c + 1) * C, :])
        dec_ref[c:c + 1, :] = decay
        ke_ref[cs, :] = (k32_ref[cs, :] * decay).astype(BF16)
    gate_ref[...] = (ng_ref[...] * (0.5 * hg * (1.0 + jnp.tanh(0.5 * hg)))).astype(BF16)

    qk = _dot(h, w_ref[:, hg_cols:hg_cols + 2 * MOBA_WIDTH])
    cos, sa, sb = cos_ref[...], sa_ref[...], sb_ref[...]
    q_scale = hd ** -0.5 * LOG2E
    for o in range(0, 2 * MOBA_WIDTH, hd):
        t = qk[:, o:o + hd]
        r = t * cos + pltpu.roll(t, hd - ROT_DIM // 2, 1) * sa + pltpu.roll(t, ROT_DIM // 2, 1) * sb
        if o < MOBA_WIDTH:
            r = r * q_scale
        pm_ref[:, o:o + hd] = r.astype(BF16)
    v = _dot(h, w_ref[:, hg_cols + 2 * MOBA_WIDTH:])
    ones_rows = jnp.ones((vt_ref.shape[2] - hd, BLK), BF16)
    for head in range(MOBA_HEADS):
        for blk in range(v.shape[0] // BLK):
            vt_ref[head, blk, :hd, :] = v[blk * BLK:(blk + 1) * BLK, head * hd:(head + 1) * hd].T.astype(BF16)
            vt_ref[head, blk, hd:, :] = ones_rows


def _inproj(xf, g, w, cos_t, sa_t, sb_t, lb, ng, cum, batch, seq):
    n = xf.shape[0]
    tm = TOKEN_TILE
    tiles_per_seq = seq // tm
    blocks_per_tile = tm // MOBA_BLOCK
    vt_rows = MOBA_HEAD_DIM + BF16_SUBLANES
    rot = lambda: pl.BlockSpec((tm, MOBA_HEAD_DIM), lambda i: (i % tiles_per_seq, 0))
    tok = lambda width: pl.BlockSpec((tm, width), lambda i: (i, 0))
    per_block = lambda rows: pl.BlockSpec(
        (None, MOBA_HEADS, blocks_per_tile, rows, MOBA_BLOCK),
        lambda i: (i // tiles_per_seq, 0, i % tiles_per_seq, 0, 0))
    hgrn_bf16 = jax.ShapeDtypeStruct((n, HGRN_WIDTH), BF16)
    return pl.pallas_call(
        _inproj_kernel,
        grid=(n // tm,),
        in_specs=[
            tok(D_MODEL),
            _const_spec((1, D_MODEL)),
            _const_spec((D_MODEL, IN_COLS)),
            rot(), rot(), rot(),
            _const_spec((1, HGRN_WIDTH)),
            _const_spec((1, HGRN_WIDTH)),
            _const_spec((HGRN_TILE, HGRN_TILE)),
        ],
        out_specs=[
            tok(HGRN_WIDTH), tok(HGRN_WIDTH), tok(HGRN_WIDTH),
            pl.BlockSpec((tm // HGRN_CHUNK, HGRN_WIDTH), lambda i: (i, 0)),
            tok(HGRN_WIDTH), tok(HGRN_WIDTH),
            tok(2 * MOBA_WIDTH),
            per_block(vt_rows),
        ],
        out_shape=[
            hgrn_bf16, hgrn_bf16, hgrn_bf16,
            jax.ShapeDtypeStruct((n // HGRN_CHUNK, HGRN_WIDTH), F32),
            hgrn_bf16, hgrn_bf16,
            jax.ShapeDtypeStruct((n, 2 * MOBA_WIDTH), BF16),
            jax.ShapeDtypeStruct((batch, MOBA_HEADS, seq // MOBA_BLOCK, vt_rows, MOBA_BLOCK), BF16),
        ],
        scratch_shapes=[pltpu.VMEM((tm, HGRN_WIDTH), F32), pltpu.VMEM((tm, HGRN_WIDTH), F32)],
        compiler_params=pltpu.CompilerParams(
            dimension_semantics=("arbitrary",), vmem_limit_bytes=VMEM_LIMIT_BYTES),
        name="inproj",
    )(xf, g, w, cos_t, sa_t, sb_t, lb, ng, cum)


def _hgrn_kernel(qt_ref, kt_ref, ke_ref, dec_ref, v_ref, gate_ref, *rest, n_cast):
    w_refs, o_ref, wb_refs = rest[:n_cast], rest[n_cast], rest[n_cast + 1:2 * n_cast + 1]
    st_ref, keb_ref, oi_ref = rest[2 * n_cast + 1:]
    for w_ref, wb_ref in zip(w_refs, wb_refs):
        wb_ref[...] = w_ref[...].astype(BF16)
    T, C, dh = HGRN_TILE, HGRN_CHUNK, HGRN_HEAD_DIM
    n_chunks = T // C

    @pl.when((pl.program_id(0) == 0) & (pl.program_id(1) == 0))
    def _():
        keb_ref[...] = jnp.zeros_like(keb_ref)

    @pl.when(pl.program_id(1) == 0)
    def _():
        st_ref[...] = jnp.zeros_like(st_ref)

    row = lax.broadcasted_iota(jnp.int32, (T, T), 0)
    col = lax.broadcasted_iota(jnp.int32, (T, T), 1)
    intra_mask = (row // C == col // C) & (col <= row)

    for u in range(HGRN_SUBTILES):
        rows = slice(u * T, (u + 1) * T)
        for h in range(HGRN_HEADS):
            hs = slice(h * dh, (h + 1) * dh)
            for c in range(n_chunks):
                keb_ref[u, h, c * C:(c + 1) * C, c * dh:(c + 1) * dh] = ke_ref[u * T + c * C:u * T + (c + 1) * C, hs]
            vb = v_ref[rows, hs]
            a = jnp.where(intra_mask, _dot_nt(qt_ref[rows, hs], kt_ref[rows, hs]), 0.0)
            upd = _dot(vb.astype(F32).T.astype(BF16), keb_ref[u, h])
            oi_ref[u, h] = _dot(a.astype(BF16), vb)
            st = st_ref[h]
            for c in range(n_chunks):
                cs = slice(c * C, (c + 1) * C)
                oi_ref[u, h, cs, :] += _dot_nt(qt_ref[u * T + c * C:u * T + (c + 1) * C, hs],
                                               st.astype(BF16))
                st = st * dec_ref[u * n_chunks + c:u * n_chunks + c + 1, hs] + upd[:, c * dh:(c + 1) * dh]
            st_ref[h] = st
            o_h = oi_ref[u, h]
            o_h = o_h * lax.rsqrt(jnp.mean(o_h * o_h, axis=-1, keepdims=True) + NORM_EPS)
            o_ref[rows, hs] = (o_h * gate_ref[rows, hs].astype(F32)).astype(o_ref.dtype)


def _hgrn(qt, kt, ke, dec, v, gate, weights, batch, seq):
    n = qt.shape[0]
    T, U = HGRN_TILE, HGRN_SUBTILES
    steps = seq // (T * U)
    n_chunks = T // HGRN_CHUNK
    tok = lambda: pl.BlockSpec((T * U, HGRN_WIDTH), lambda b, i: (b * steps + i, 0))
    assert all(w.shape[0] % (batch * steps * BF16_SUBLANES) == 0 for w in weights)
    row_block = lambda w: pl.BlockSpec((w.shape[0] // (batch * steps), w.shape[1]),
                                       lambda b, i: (b * steps + i, 0))
    outs = pl.pallas_call(
        functools.partial(_hgrn_kernel, n_cast=len(weights)),
        grid=(batch, steps),
        in_specs=[
            tok(), tok(), tok(),
            pl.BlockSpec((U * n_chunks, HGRN_WIDTH), lambda b, i: (b * steps + i, 0)),
            tok(), tok(),
        ] + [row_block(w) for w in weights],
        out_specs=[tok()] + [row_block(w) for w in weights],
        out_shape=[jax.ShapeDtypeStruct((n, HGRN_WIDTH), BF16)]
        + [jax.ShapeDtypeStruct(w.shape, BF16) for w in weights],
        scratch_shapes=[
            pltpu.VMEM((HGRN_HEADS, HGRN_HEAD_DIM, HGRN_HEAD_DIM), F32),
            pltpu.VMEM((U, HGRN_HEADS, T, n_chunks * HGRN_HEAD_DIM), BF16),
            pltpu.VMEM((U, HGRN_HEADS, T, HGRN_HEAD_DIM), F32),
        ],
        compiler_params=pltpu.CompilerParams(
            dimension_semantics=("arbitrary", "arbitrary"), vmem_limit_bytes=VMEM_LIMIT_BYTES),
        name="hgrn",
    )(qt, kt, ke, dec, v, gate, *weights)
    return outs[0], outs[1:]


def _moba_kernel(q_ref, k_ref, vt_ref, o_ref, kmean_ref, sel_ref, acc_ref, s_ref):
    BLK, hd, H = MOBA_BLOCK, MOBA_HEAD_DIM, MOBA_HEADS
    TQ = q_ref.shape[0]
    n_blocks = k_ref.shape[0] // BLK
    i = pl.program_id(1)
    own0 = 2 * i

    @pl.when(i == 0)
    def _():
        for j in range(n_blocks):
            kb = k_ref[j * BLK:(j + 1) * BLK, :].astype(F32)
            kmean_ref[j:j + 1, :] = jnp.sum(kb, axis=0, keepdims=True) * (1.0 / BLK)

    blk_id = lax.broadcasted_iota(jnp.int32, (n_blocks, TQ), 0)
    second = lax.broadcasted_iota(jnp.int32, (n_blocks, TQ), 1) >= BLK
    past = blk_id < own0 + second.astype(jnp.int32)
    k_pos = lax.broadcasted_iota(jnp.int32, (BLK, TQ), 0)
    q_lane = lax.broadcasted_iota(jnp.int32, (1, TQ), 1)

    def scores_into(slot, j, h):
        rows = pl.ds(pl.multiple_of(jnp.minimum(j, n_blocks - 1) * BLK, BLK), BLK)
        hs = slice(h * hd, (h + 1) * hd)
        s_ref[slot, h] = _dot_nt(k_ref[rows, hs], q_ref[:, hs])

    PAIR_A, PAIR_B = (0, 1), (2, 3)
    SLOT_OWN = PAIR_B
    gates = [_dot_nt(kmean_ref[:, h * hd:(h + 1) * hd].astype(BF16), q_ref[:, h * hd:(h + 1) * hd])
             for h in range(H)]
    for h in range(H):
        scores_into(SLOT_OWN[0], own0, h)
        scores_into(SLOT_OWN[1], own0 + 1, h)
    for h in range(H):
        scores_into(PAIR_A[0], 0, h)
        scores_into(PAIR_A[1], 1, h)

    m_init = []
    for h in range(H):
        g = jnp.where(past, gates[h], -jnp.inf)
        chosen = jnp.zeros(g.shape, F32)
        for _ in range(MOBA_TOP_K):
            m = jnp.max(g, axis=0, keepdims=True)
            idx = jnp.min(jnp.where(g == m, blk_id, n_blocks), axis=0, keepdims=True)
            hit = blk_id == idx
            chosen = jnp.where(hit, 1.0, chosen)
            g = jnp.where(hit, -jnp.inf, g)
        sel_ref[h] = jnp.where(past, chosen, 0.0)
        picked0 = sel_ref[h, pl.ds(own0, 1), :] > 0.0
        limit0 = jnp.where(q_lane < BLK, q_lane, jnp.where(picked0, BLK, -1))
        limit1 = jnp.where(q_lane < BLK, -1, q_lane - BLK)
        s0 = jnp.where(k_pos <= limit0, s_ref[SLOT_OWN[0], h], NEG_BIG)
        s1 = jnp.where(k_pos <= limit1, s_ref[SLOT_OWN[1], h], NEG_BIG)
        m0 = jnp.maximum(jnp.max(s0, axis=0, keepdims=True), jnp.max(s1, axis=0, keepdims=True))
        acc_ref[h] = (_dot(vt_ref[h, own0], jnp.exp2(s0 - m0).astype(BF16))
                      + _dot(vt_ref[h, own0 + 1], jnp.exp2(s1 - m0).astype(BF16)))
        m_init.append(m0)

    def consume_pair(slots, j, h, m):
        sx, sy = s_ref[slots[0], h], s_ref[slots[1], h]
        px = sel_ref[h, pl.ds(j, 1), :] > 0.0
        py = sel_ref[h, pl.ds(j + 1, 1), :] > 0.0
        m_new = jnp.maximum(m, jnp.maximum(
            jnp.where(px, jnp.max(sx, axis=0, keepdims=True), NEG_BIG),
            jnp.where(py, jnp.max(sy, axis=0, keepdims=True), NEG_BIG)))
        ex = jnp.exp2(sx - jnp.where(px, m_new, -NEG_BIG)).astype(BF16)
        ey = jnp.exp2(sy - jnp.where(py, m_new, -NEG_BIG)).astype(BF16)
        acc_ref[h] = (jnp.exp2(m - m_new) * acc_ref[h]
                      + (_dot(vt_ref[h, j], ex) + _dot(vt_ref[h, j + 1], ey)))
        return m_new

    def four_blocks(base, ms):
        ms = list(ms)

        def produce(pair, h):
            j = base + 2 if pair == "B" else base + 4
            slots = PAIR_B if pair == "B" else PAIR_A
            scores_into(slots[0], j, h)
            scores_into(slots[1], j + 1, h)

        def consume(pair, h):
            j = base if pair == "A" else base + 2
            ms[h] = consume_pair(PAIR_A if pair == "A" else PAIR_B, j, h, ms[h])

        for op in "PB0 PB1 CA0 PB2 CA1 PB3 CA2 PA0 CA3 PA1 CB0 PA2 CB1 PA3 CB2 CB3".split():
            (produce if op[0] == "P" else consume)(op[1], int(op[2]))
        return tuple(ms)

    long_trips = i // 4
    ms = lax.fori_loop(0, long_trips,
                       lambda t, ms: four_blocks(8 * t + 4, four_blocks(8 * t, ms)), tuple(m_init))
    ms = lax.fori_loop(0, (i % 4) // 2, lambda t, ms: four_blocks(8 * long_trips, ms), ms)

    @pl.when(i % 2 == 1)
    def _():
        for h in range(H):
            consume_pair(PAIR_A, 2 * (i - 1), h, ms[h])

    for h in range(H):
        o = acc_ref[h, :hd, :] / acc_ref[h, hd:hd + 1, :]
        o_ref[:, h * hd:(h + 1) * hd] = o.T.astype(o_ref.dtype)


def _moba(pm, vt, batch, seq):
    n = pm.shape[0]
    BLK, hd, H = MOBA_BLOCK, MOBA_HEAD_DIM, MOBA_HEADS
    TQ = MOBA_QUERY_BLOCKS * BLK
    n_blocks = seq // BLK
    steps = seq // TQ
    vt_rows = vt.shape[-2]
    return pl.pallas_call(
        _moba_kernel,
        grid=(batch, steps),
        in_specs=[
            pl.BlockSpec((TQ, H * hd), lambda b, i: (b * steps + i, 0)),
            pl.BlockSpec((seq, H * hd), lambda b, i: (b, 1)),
            pl.BlockSpec((None, H, n_blocks, vt_rows, BLK), lambda b, i: (b, 0, 0, 0, 0)),
        ],
        out_specs=pl.BlockSpec((TQ, H * hd), lambda b, i: (b * steps + i, 0)),
        out_shape=jax.ShapeDtypeStruct((n, MOBA_WIDTH), BF16),
        scratch_shapes=[
            pltpu.VMEM((n_blocks, H * hd), F32),
            pltpu.VMEM((H, n_blocks, TQ), F32),
            pltpu.VMEM((H, vt_rows, TQ), F32),
            pltpu.VMEM((4, H, BLK, TQ), F32),
        ],
        compiler_params=pltpu.CompilerParams(
            dimension_semantics=("arbitrary", "arbitrary"),
            vmem_limit_bytes=VMEM_LIMIT_BYTES),
        name="moba",
    )(pm, pm, vt)


def _mix_xattn_kernel(x_ref, oh_ref, om_ref, wo_ref, g_ref, wq_ref, mem_ref, gm_ref, wk_ref, wv_ref,
                      wxo_ref, y_ref, k_ref, v_ref):
    @pl.when(pl.program_id(1) == 0)
    def _():
        hm = _rms(mem_ref[...], gm_ref[...]).astype(BF16)
        k_ref[...] = _dot(hm, wk_ref[...]).astype(BF16)
        v_ref[...] = _dot(hm, wv_ref[...]).astype(BF16)

    x1 = (x_ref[...] + _dot(oh_ref[...], wo_ref[:HGRN_WIDTH, :])
          + _dot(om_ref[...], wo_ref[HGRN_WIDTH:, :]))
    h = _rms(x1, g_ref[...]).astype(BF16)
    q = (_dot(h, wq_ref[...]) * (XATTN_HEAD_DIM ** -0.5 * LOG2E)).astype(BF16)
    outs = []
    for hd in range(XATTN_HEADS):
        hs = slice(hd * XATTN_HEAD_DIM, (hd + 1) * XATTN_HEAD_DIM)
        s = _dot_nt(q[:, hs], k_ref[:, hs])
        p = jnp.exp2(s - jnp.max(s, axis=-1, keepdims=True))
        l = jnp.sum(p, axis=-1, keepdims=True)
        outs.append((_dot(p.astype(BF16), v_ref[:, hs]) / l).astype(BF16))
    o = jnp.concatenate(outs, axis=-1)
    y_ref[...] = x1 + _dot(o, wxo_ref[...])


def _mix_xattn(xf, oh, om, wo, g, wq, memf, gm, wk, wv, wxo, batch, seq):
    n = xf.shape[0]
    tm = TOKEN_TILE
    steps = seq // tm
    tok = lambda w: pl.BlockSpec((tm, w), lambda b, i: (b * steps + i, 0))
    return pl.pallas_call(
        _mix_xattn_kernel,
        grid=(batch, steps),
        in_specs=[
            tok(D_MODEL), tok(HGRN_WIDTH), tok(MOBA_WIDTH),
            _const_spec((D_MODEL, D_MODEL)),
            _const_spec((1, D_MODEL)),
            _const_spec((D_MODEL, D_MODEL)),
            pl.BlockSpec((MEM_LEN, D_MODEL), lambda b, i: (b, 0)),
            _const_spec((1, D_MODEL)),
            _const_spec((D_MODEL, D_MODEL)),
            _const_spec((D_MODEL, D_MODEL)),
            _const_spec((D_MODEL, D_MODEL)),
        ],
        out_specs=tok(D_MODEL),
        out_shape=jax.ShapeDtypeStruct((n, D_MODEL), F32),
        scratch_shapes=[pltpu.VMEM((MEM_LEN, D_MODEL), BF16), pltpu.VMEM((MEM_LEN, D_MODEL), BF16)],
        compiler_params=pltpu.CompilerParams(
            dimension_semantics=("arbitrary", "arbitrary"), vmem_limit_bytes=VMEM_LIMIT_BYTES),
        name="mix_xattn",
    )(xf, oh, om, wo, g, wq, memf, gm, wk, wv, wxo)


def _mlp_kernel(x_ref, g_ref, w1_ref, w2_ref, gf_ref, y_ref, *, final_norm):
    x = x_ref[...]
    h = _rms(x, g_ref[...]).astype(BF16)
    ff_chunk = D_MODEL
    acc = x
    for c in range(D_FF // ff_chunk):
        cs = slice(c * ff_chunk, (c + 1) * ff_chunk)
        z = jnp.maximum(_dot(h, w1_ref[:, cs]), 0.0)
        acc = acc + _dot((z * z).astype(BF16), w2_ref[cs, :])
    y_ref[...] = _rms(acc, gf_ref[...]) if final_norm else acc


def _mlp(xf, g, w1, w2, gf, final_norm):
    n = xf.shape[0]
    tm = TOKEN_TILE
    return pl.pallas_call(
        functools.partial(_mlp_kernel, final_norm=final_norm),
        grid=(n // tm,),
        in_specs=[
            pl.BlockSpec((tm, D_MODEL), lambda i: (i, 0)),
            _const_spec((1, D_MODEL)),
            _const_spec((D_MODEL, D_FF)),
            _const_spec((D_FF, D_MODEL)),
            _const_spec((1, D_MODEL)),
        ],
        out_specs=pl.BlockSpec((tm, D_MODEL), lambda i: (i, 0)),
        out_shape=jax.ShapeDtypeStruct((n, D_MODEL), F32),
        compiler_params=pltpu.CompilerParams(
            dimension_semantics=("arbitrary",), vmem_limit_bytes=VMEM_LIMIT_BYTES),
        name="mlp",
    )(xf, g, w1, w2, gf)


def _rotary_tables(seq):
    half = ROT_DIM // 2
    inv_freq = np.float32(ROPE_THETA) ** (-np.arange(half, dtype=np.float32) * np.float32(2.0 / ROT_DIM))
    ang = (np.arange(seq, dtype=np.float32)[:, None] * inv_freq[None, :]).astype(np.float64)
    cos, sin = np.cos(ang), np.sin(ang)
    rest = MOBA_HEAD_DIM - ROT_DIM
    cos_t = np.concatenate([cos, cos, np.ones((seq, rest))], axis=-1)
    sa_t = np.concatenate([-sin, np.zeros((seq, MOBA_HEAD_DIM - half))], axis=-1)
    sb_t = np.concatenate([np.zeros((seq, half)), sin, np.zeros((seq, rest))], axis=-1)
    return tuple(jnp.asarray(t, F32) for t in (cos_t, sa_t, sb_t))


def _cumsum_matrix():
    T, C = HGRN_TILE, HGRN_CHUNK
    r = np.arange(T)
    same = (r[:, None] // C) == (r[None, :] // C)
    return jnp.asarray(same & (r[None, :] <= r[:, None]), BF16)


def kernel(x, mem, norm_mix, w_in, lb_logits, hgrn_norm, w_out, norm_xattn, norm_mem,
           w_xq, w_xk, w_xv, w_xo, norm_mlp, w_ff1, w_ff2, norm_final):
    batch, seq, d_model = x.shape
    assert d_model == D_MODEL and mem.shape == (batch, MEM_LEN, D_MODEL), (x.shape, mem.shape)
    assert w_in.shape[1:] == (D_MODEL, IN_COLS) and w_ff1.shape[1:] == (D_MODEL, D_FF), (w_in.shape, w_ff1.shape)
    assert seq % TOKEN_TILE == 0 and seq % (HGRN_TILE * HGRN_SUBTILES) == 0, seq
    assert seq % (MOBA_BLOCK * MOBA_QUERY_BLOCKS) == 0 and MOBA_QUERY_BLOCKS == 2, seq
    n = batch * seq
    xf = x.reshape(n, D_MODEL)
    lb_table = jnp.cumsum(jax.nn.softmax(lb_logits.astype(F32), axis=0), axis=0)
    cos_t, sa_t, sb_t = _rotary_tables(seq)
    cum = _cumsum_matrix()
    row = lambda v: v.reshape(1, -1)
    depth = norm_mix.shape[0]
    for l in range(depth):
        qt, kt, ke, dec, hv, gate, pm, vt = _inproj(
            xf, row(norm_mix[l]), w_in[l].astype(BF16), cos_t, sa_t, sb_t,
            row(lb_table[l]), row(hgrn_norm[l]), cum, batch, seq)
        o_hgrn, (wo, wq, wk, wv, wxo, w1, w2) = _hgrn(
            qt, kt, ke, dec, hv, gate,
            (w_out[l], w_xq[l], w_xk[l], w_xv[l], w_xo[l], w_ff1[l], w_ff2[l]), batch, seq)
        o_moba = _moba(pm, vt, batch, seq)
        x2 = _mix_xattn(xf, o_hgrn, o_moba, wo, row(norm_xattn[l]), wq,
                        mem.reshape(-1, D_MODEL), row(norm_mem[l]), wk, wv, wxo, batch, seq)
        xf = _mlp(x2, row(norm_mlp[l]), w1, w2, row(norm_final), final_norm=(l == depth - 1))
    return xf.reshape(batch, seq, D_MODEL)
```

```python
import functools
import math

import jax
import jax.numpy as jnp
import numpy as np
from jax import lax
from jax.experimental import pallas as pl
from jax.experimental.pallas import tpu as pltpu

F32 = jnp.float32
BF16 = jnp.bfloat16

D_MODEL = 1024
MEM_LEN = 256
HGRN_HEADS = 4
HGRN_WIDTH = 512
HGRN_HEAD_DIM = 128
HGRN_CHUNK = 32
MOBA_HEADS = 4
MOBA_WIDTH = 512
MOBA_HEAD_DIM = 128
MOBA_BLOCK = 256
MOBA_TOP_K = 3
MOBA_QUERY_BLOCKS = 2
ROPE_THETA = 500000.0
ROT_DIM = 32
XATTN_HEADS = 4
XATTN_HEAD_DIM = 256
D_FF = 4096
NORM_EPS = 1e-6
IN_COLS = 4 * HGRN_WIDTH + 3 * MOBA_WIDTH

LOG2E = math.log2(math.e)
NEG_BIG = -1e30

VMEM_LIMIT_BYTES = 56 * 1024 * 1024

BF16_SUBLANES = 16
HGRN_TILE = 256
HGRN_SUBTILES = 4
TOKEN_TILE = 1024


def _rms(x, g):
    return x * lax.rsqrt(jnp.mean(x * x, axis=-1, keepdims=True) + NORM_EPS) * g


def _dot(a, b):
    return jnp.dot(a, b, preferred_element_type=F32)


def _dot_nt(a, b):
    return lax.dot_general(a, b, (((1,), (1,)), ((), ())), preferred_element_type=F32)


def _const_spec(shape):
    zeros = (0,) * len(shape)
    return pl.BlockSpec(shape, lambda *_: zeros, pipeline_mode=pl.Buffered(1))


def _inproj_kernel(x_ref, g_ref, w_ref, cos_ref, sa_ref, sb_ref, lb_ref, ng_ref, cum_ref,
                   qt_ref, kt_ref, ke_ref, dec_ref, v_ref, gate_ref, pm_ref, vt_ref, b_ref, k32_ref):
    h = _rms(x_ref[...], g_ref[...]).astype(BF16)
    W, C, T = HGRN_WIDTH, HGRN_CHUNK, HGRN_TILE
    hg_cols = 4 * W
    hd, BLK = MOBA_HEAD_DIM, MOBA_BLOCK
    tm = x_ref.shape[0]

    lb = lb_ref[...]
    hf = _dot(h, w_ref[:, W:2 * W])
    hq = _dot(h, w_ref[:, 0:W])
    hg = _dot(h, w_ref[:, 3 * W:4 * W])
    v_ref[...] = _dot(h, w_ref[:, 2 * W:3 * W]).astype(BF16)
    f = 0.5 * (1.0 + lb) + (0.5 * (1.0 - lb)) * jnp.tanh(0.5 * hf)
    logf = jnp.log2(f)
    l1 = logf.astype(BF16)
    l2 = (logf - l1.astype(F32)).astype(BF16)
    cum = cum_ref[...]
    for t in range(tm // T):
        rows = slice(t * T, (t + 1) * T)
        b_ref[rows, :] = _dot(cum, l1[rows, :]) + _dot(cum, l2[rows, :])
    b = b_ref[...]
    qt_ref[...] = (hq * jnp.exp2(b)).astype(BF16)
    k32 = (1.0 - f) * jnp.exp2(-b)
    kt_ref[...] = k32.astype(BF16)
    k32_ref[...] = k32
    for c in range(tm // C):
        cs = slice(c * C, (c + 1) * C)
        decay = jnp.exp2(b_ref[(c + 1) * C - 1:(c + 1) * C, :])
        dec_ref[c:c + 1, :] = decay
        ke_ref[cs, :] = (k32_ref[cs, :] * decay).astype(BF16)
    gate_ref[...] = (ng_ref[...] * (0.5 * hg * (1.0 + jnp.tanh(0.5 * hg)))).astype(BF16)

    qk = _dot(h, w_ref[:, hg_cols:hg_cols + 2 * MOBA_WIDTH])
    cos, sa, sb = cos_ref[...], sa_ref[...], sb_ref[...]
    q_scale = hd ** -0.5 * LOG2E
    for o in range(0, 2 * MOBA_WIDTH, hd):
        t = qk[:, o:o + hd]
        r = t * cos + pltpu.roll(t, hd - ROT_DIM // 2, 1) * sa + pltpu.roll(t, ROT_DIM // 2, 1) * sb
        if o < MOBA_WIDTH:
            r = r * q_scale
        pm_ref[:, o:o + hd] = r.astype(BF16)
    v = _dot(h, w_ref[:, hg_cols + 2 * MOBA_WIDTH:])
    ones_rows = jnp.ones((vt_ref.shape[2] - hd, BLK), BF16)
    for head in range(MOBA_HEADS):
        for blk in range(v.shape[0] // BLK):
            vt_ref[head, blk, :hd, :] = v[blk * BLK:(blk + 1) * BLK, head * hd:(head + 1) * hd].T.astype(BF16)
            vt_ref[head, blk, hd:, :] = ones_rows


def _inproj(xf, g, w, cos_t, sa_t, sb_t, lb, ng, cum, batch, seq):
    n = xf.shape[0]
    tm = TOKEN_TILE
    tiles_per_seq = seq // tm
    blocks_per_tile = tm // MOBA_BLOCK
    vt_rows = MOBA_HEAD_DIM + BF16_SUBLANES
    rot = lambda: pl.BlockSpec((tm, MOBA_HEAD_DIM), lambda i: (i % tiles_per_seq, 0))
    tok = lambda width: pl.BlockSpec((tm, width), lambda i: (i, 0))
    per_block = lambda rows: pl.BlockSpec(
        (None, MOBA_HEADS, blocks_per_tile, rows, MOBA_BLOCK),
        lambda i: (i // tiles_per_seq, 0, i % tiles_per_seq, 0, 0))
    hgrn_bf16 = jax.ShapeDtypeStruct((n, HGRN_WIDTH), BF16)
    return pl.pallas_call(
        _inproj_kernel,
        grid=(n // tm,),
        in_specs=[
            tok(D_MODEL),
            _const_spec((1, D_MODEL)),
            _const_spec((D_MODEL, IN_COLS)),
            rot(), rot(), rot(),
            _const_spec((1, HGRN_WIDTH)),
            _const_spec((1, HGRN_WIDTH)),
            _const_spec((HGRN_TILE, HGRN_TILE)),
        ],
        out_specs=[
            tok(HGRN_WIDTH), tok(HGRN_WIDTH), tok(HGRN_WIDTH),
            pl.BlockSpec((tm // HGRN_CHUNK, HGRN_WIDTH), lambda i: (i, 0)),
            tok(HGRN_WIDTH), tok(HGRN_WIDTH),
            tok(2 * MOBA_WIDTH),
            per_block(vt_rows),
        ],
        out_shape=[
            hgrn_bf16, hgrn_bf16, hgrn_bf16,
            jax.ShapeDtypeStruct((n // HGRN_CHUNK, HGRN_WIDTH), F32),
            hgrn_bf16, hgrn_bf16,
            jax.ShapeDtypeStruct((n, 2 * MOBA_WIDTH), BF16),
            jax.ShapeDtypeStruct((batch, MOBA_HEADS, seq // MOBA_BLOCK, vt_rows, MOBA_BLOCK), BF16),
        ],
        scratch_shapes=[pltpu.VMEM((tm, HGRN_WIDTH), F32), pltpu.VMEM((tm, HGRN_WIDTH), F32)],
        compiler_params=pltpu.CompilerParams(
            dimension_semantics=("arbitrary",), vmem_limit_bytes=VMEM_LIMIT_BYTES),
        name="inproj",
    )(xf, g, w, cos_t, sa_t, sb_t, lb, ng, cum)


def _hgrn_kernel(qt_ref, kt_ref, ke_ref, dec_ref, v_ref, gate_ref, *rest, n_cast):
    w_refs, o_ref, wb_refs = rest[:n_cast], rest[n_cast], rest[n_cast + 1:2 * n_cast + 1]
    st_ref, keb_ref, oi_ref = rest[2 * n_cast + 1:]
    for w_ref, wb_ref in zip(w_refs, wb_refs):
        wb_ref[...] = w_ref[...].astype(BF16)
    T, C, dh = HGRN_TILE, HGRN_CHUNK, HGRN_HEAD_DIM
    n_chunks = T // C

    @pl.when((pl.program_id(0) == 0) & (pl.program_id(1) == 0))
    def _():
        keb_ref[...] = jnp.zeros_like(keb_ref)

    @pl.when(pl.program_id(1) == 0)
    def _():
        st_ref[...] = jnp.zeros_like(st_ref)

    row = lax.broadcasted_iota(jnp.int32, (T, T), 0)
    col = lax.broadcasted_iota(jnp.int32, (T, T), 1)
    intra_mask = (row // C == col // C) & (col <= row)

    for u in range(HGRN_SUBTILES):
        rows = slice(u * T, (u + 1) * T)
        for h in range(HGRN_HEADS):
            hs = slice(h * dh, (h + 1) * dh)
            for c in range(n_chunks):
                keb_ref[u, h, c * C:(c + 1) * C, c * dh:(c + 1) * dh] = ke_ref[u * T + c * C:u * T + (c + 1) * C, hs]
            vb = v_ref[rows, hs]
            a = jnp.where(intra_mask, _dot_nt(qt_ref[rows, hs], kt_ref[rows, hs]), 0.0)
            upd = _dot(vb.astype(F32).T.astype(BF16), keb_ref[u, h])
            oi_ref[u, h] = _dot(a.astype(BF16), vb)
            st = st_ref[h]
            for c in range(n_chunks):
                cs = slice(c * C, (c + 1) * C)
                oi_ref[u, h, cs, :] += _dot_nt(qt_ref[u * T + c * C:u * T + (c + 1) * C, hs],
                                               st.astype(BF16))
                st = st * dec_ref[u * n_chunks + c:u * n_chunks + c + 1, hs] + upd[:, c * dh:(c + 1) * dh]
            st_ref[h] = st
            o_h = oi_ref[u, h]
            o_h = o_h * lax.rsqrt(jnp.mean(o_h * o_h, axis=-1, keepdims=True) + NORM_EPS)
            o_ref[rows, hs] = (o_h * gate_ref[rows, hs].astype(F32)).astype(o_ref.dtype)


def _hgrn(qt, kt, ke, dec, v, gate, weights, batch, seq):
    n = qt.shape[0]
    T, U = HGRN_TILE, HGRN_SUBTILES
    steps = seq // (T * U)
    n_chunks = T // HGRN_CHUNK
    tok = lambda: pl.BlockSpec((T * U, HGRN_WIDTH), lambda b, i: (b * steps + i, 0))
    assert all(w.shape[0] % (batch * steps * BF16_SUBLANES) == 0 for w in weights)
    row_block = lambda w: pl.BlockSpec((w.shape[0] // (batch * steps), w.shape[1]),
                                       lambda b, i: (b * steps + i, 0))
    outs = pl.pallas_call(
        functools.partial(_hgrn_kernel, n_cast=len(weights)),
        grid=(batch, steps),
        in_specs=[
            tok(), tok(), tok(),
            pl.BlockSpec((U * n_chunks, HGRN_WIDTH), lambda b, i: (b * steps + i, 0)),
            tok(), tok(),
        ] + [row_block(w) for w in weights],
        out_specs=[tok()] + [row_block(w) for w in weights],
        out_shape=[jax.ShapeDtypeStruct((n, HGRN_WIDTH), BF16)]
        + [jax.ShapeDtypeStruct(w.shape, BF16) for w in weights],
        scratch_shapes=[
            pltpu.VMEM((HGRN_HEADS, HGRN_HEAD_DIM, HGRN_HEAD_DIM), F32),
            pltpu.VMEM((U, HGRN_HEADS, T, n_chunks * HGRN_HEAD_DIM), BF16),
            pltpu.VMEM((U, HGRN_HEADS, T, HGRN_HEAD_DIM), F32),
        ],
        compiler_params=pltpu.CompilerParams(
            dimension_semantics=("arbitrary", "arbitrary"), vmem_limit_bytes=VMEM_LIMIT_BYTES),
        name="hgrn",
    )(qt, kt, ke, dec, v, gate, *weights)
    return outs[0], outs[1:]


def _moba_kernel(q_ref, k_ref, vt_ref, o_ref, kmean_ref, sel_ref, acc_ref, s_ref):
    BLK, hd, H = MOBA_BLOCK, MOBA_HEAD_DIM, MOBA_HEADS
    TQ = q_ref.shape[0]
    n_blocks = k_ref.shape[0] // BLK
    i = pl.program_id(1)
    own0 = 2 * i

    @pl.when(i == 0)
    def _():
        for j in range(n_blocks):
            kb = k_ref[j * BLK:(j + 1) * BLK, :].astype(F32)
            kmean_ref[j:j + 1, :] = jnp.sum(kb, axis=0, keepdims=True) * (1.0 / BLK)

    blk_id = lax.broadcasted_iota(jnp.int32, (n_blocks, TQ), 0)
    second = lax.broadcasted_iota(jnp.int32, (n_blocks, TQ), 1) >= BLK
    past = blk_id < own0 + second.astype(jnp.int32)
    k_pos = lax.broadcasted_iota(jnp.int32, (BLK, TQ), 0)
    q_lane = lax.broadcasted_iota(jnp.int32, (1, TQ), 1)

    def scores_into(slot, j, h):
        rows = pl.ds(pl.multiple_of(jnp.minimum(j, n_blocks - 1) * BLK, BLK), BLK)
        hs = slice(h * hd, (h + 1) * hd)
        s_ref[slot, h] = _dot_nt(k_ref[rows, hs], q_ref[:, hs]).astype(BF16)

    PAIR_A, PAIR_B = (0, 1), (2, 3)
    SLOT_OWN = PAIR_B
    gates = [_dot_nt(kmean_ref[:, h * hd:(h + 1) * hd].astype(BF16), q_ref[:, h * hd:(h + 1) * hd])
             for h in range(H)]
    for h in range(H):
        scores_into(SLOT_OWN[0], own0, h)
        scores_into(SLOT_OWN[1], own0 + 1, h)
    for h in range(H):
        scores_into(PAIR_A[0], 0, h)
        scores_into(PAIR_A[1], 1, h)

    m_init = []
    for h in range(H):
        g = jnp.where(past, gates[h], -jnp.inf)
        chosen = jnp.zeros(g.shape, F32)
        for _ in range(MOBA_TOP_K):
            m = jnp.max(g, axis=0, keepdims=True)
            idx = jnp.min(jnp.where(g == m, blk_id, n_blocks), axis=0, keepdims=True)
            hit = blk_id == idx
            chosen = jnp.where(hit, 1.0, chosen)
            g = jnp.where(hit, -jnp.inf, g)
        sel_ref[h] = jnp.where(past, chosen, 0.0)
        picked0 = sel_ref[h, pl.ds(own0, 1), :] > 0.0
        limit0 = jnp.where(q_lane < BLK, q_lane, jnp.where(picked0, BLK, -1))
        limit1 = jnp.where(q_lane < BLK, -1, q_lane - BLK)
        neg = jnp.asarray(NEG_BIG, BF16)
        s0 = jnp.where(k_pos <= limit0, s_ref[SLOT_OWN[0], h], neg)
        s1 = jnp.where(k_pos <= limit1, s_ref[SLOT_OWN[1], h], neg)
        m0 = jnp.maximum(jnp.max(s0, axis=0, keepdims=True), jnp.max(s1, axis=0, keepdims=True))
        acc_ref[h] = _dot(vt_ref[h, own0], jnp.exp2(s0 - m0)) + _dot(vt_ref[h, own0 + 1], jnp.exp2(s1 - m0))
        m_init.append(m0.astype(F32))

    def consume_pair(slots, j, h, m):
        sx, sy = s_ref[slots[0], h], s_ref[slots[1], h]
        px = sel_ref[h, pl.ds(j, 1), :] > 0.0
        py = sel_ref[h, pl.ds(j + 1, 1), :] > 0.0
        m_new = jnp.maximum(m, jnp.maximum(
            jnp.where(px, jnp.max(sx, axis=0, keepdims=True).astype(F32), NEG_BIG),
            jnp.where(py, jnp.max(sy, axis=0, keepdims=True).astype(F32), NEG_BIG)))
        ex = jnp.exp2(sx - jnp.where(px, m_new, -NEG_BIG).astype(BF16))
        ey = jnp.exp2(sy - jnp.where(py, m_new, -NEG_BIG).astype(BF16))
        acc_ref[h] = (jnp.exp2(m - m_new) * acc_ref[h]
                      + (_dot(vt_ref[h, j], ex) + _dot(vt_ref[h, j + 1], ey)))
        return m_new

    def four_blocks(base, ms):
        ms = list(ms)

        def produce(pair, h):
            j = base + 2 if pair == "B" else base + 4
            slots = PAIR_B if pair == "B" else PAIR_A
            scores_into(slots[0], j, h)
            scores_into(slots[1], j + 1, h)

        def consume(pair, h):
            j = base if pair == "A" else base + 2
            ms[h] = consume_pair(PAIR_A if pair == "A" else PAIR_B, j, h, ms[h])

        for op in "PB0 PB1 CA0 PB2 CA1 PB3 CA2 PA0 CA3 PA1 CB0 PA2 CB1 PA3 CB2 CB3".split():
            (produce if op[0] == "P" else consume)(op[1], int(op[2]))
        return tuple(ms)

    long_trips = i // 4
    ms = lax.fori_loop(0, long_trips,
                       lambda t, ms: four_blocks(8 * t + 4, four_blocks(8 * t, ms)), tuple(m_init))
    ms = lax.fori_loop(0, (i % 4) // 2, lambda t, ms: four_blocks(8 * long_trips, ms), ms)

    @pl.when(i % 2 == 1)
    def _():
        for h in range(H):
            consume_pair(PAIR_A, 2 * (i - 1), h, ms[h])

    for h in range(H):
        o = acc_ref[h, :hd, :] / acc_ref[h, hd:hd + 1, :]
        o_ref[:, h * hd:(h + 1) * hd] = o.T.astype(o_ref.dtype)


def _moba(pm, vt, batch, seq):
    n = pm.shape[0]
    BLK, hd, H = MOBA_BLOCK, MOBA_HEAD_DIM, MOBA_HEADS
    TQ = MOBA_QUERY_BLOCKS * BLK
    n_blocks = seq // BLK
    steps = seq // TQ
    vt_rows = vt.shape[-2]
    return pl.pallas_call(
        _moba_kernel,
        grid=(batch, steps),
        in_specs=[
            pl.BlockSpec((TQ, H * hd), lambda b, i: (b * steps + i, 0)),
            pl.BlockSpec((seq, H * hd), lambda b, i: (b, 1)),
            pl.BlockSpec((None, H, n_blocks, vt_rows, BLK), lambda b, i: (b, 0, 0, 0, 0)),
        ],
        out_specs=pl.BlockSpec((TQ, H * hd), lambda b, i: (b * steps + i, 0)),
        out_shape=jax.ShapeDtypeStruct((n, MOBA_WIDTH), BF16),
        scratch_shapes=[
            pltpu.VMEM((n_blocks, H * hd), F32),
            pltpu.VMEM((H, n_blocks, TQ), F32),
            pltpu.VMEM((H, vt_rows, TQ), F32),
            pltpu.VMEM((4, H, BLK, TQ), BF16),
        ],
        compiler_params=pltpu.CompilerParams(
            dimension_semantics=("arbitrary", "arbitrary"),
            vmem_limit_bytes=VMEM_LIMIT_BYTES),
        name="moba",
    )(pm, pm, vt)


def _mix_xattn_kernel(x_ref, oh_ref, om_ref, wo_ref, g_ref, wq_ref, mem_ref, gm_ref, wk_ref, wv_ref,
                      wxo_ref, y_ref, k_ref, v_ref):
    @pl.when(pl.program_id(1) == 0)
    def _():
        hm = _rms(mem_ref[...], gm_ref[...]).astype(BF16)
        k_ref[...] = _dot(hm, wk_ref[...]).astype(BF16)
        v_ref[...] = _dot(hm, wv_ref[...]).astype(BF16)

    x1 = (x_ref[...] + _dot(oh_ref[...], wo_ref[:HGRN_WIDTH, :])
          + _dot(om_ref[...], wo_ref[HGRN_WIDTH:, :]))
    h = _rms(x1, g_ref[...]).astype(BF16)
    q = (_dot(h, wq_ref[...]) * (XATTN_HEAD_DIM ** -0.5 * LOG2E)).astype(BF16)
    outs = []
    for hd in range(XATTN_HEADS):
        hs = slice(hd * XATTN_HEAD_DIM, (hd + 1) * XATTN_HEAD_DIM)
        s = _dot_nt(q[:, hs], k_ref[:, hs])
        p = jnp.exp2(s - jnp.max(s, axis=-1, keepdims=True))
        l = jnp.sum(p, axis=-1, keepdims=True)
        outs.append((_dot(p.astype(BF16), v_ref[:, hs]) / l).astype(BF16))
    o = jnp.concatenate(outs, axis=-1)
    y_ref[...] = x1 + _dot(o, wxo_ref[...])


def _mix_xattn(xf, oh, om, wo, g, wq, memf, gm, wk, wv, wxo, batch, seq):
    n = xf.shape[0]
    tm = TOKEN_TILE
    steps = seq // tm
    tok = lambda w: pl.BlockSpec((tm, w), lambda b, i: (b * steps + i, 0))
    return pl.pallas_call(
        _mix_xattn_kernel,
        grid=(batch, steps),
        in_specs=[
            tok(D_MODEL), tok(HGRN_WIDTH), tok(MOBA_WIDTH),
            _const_spec((D_MODEL, D_MODEL)),
            _const_spec((1, D_MODEL)),
            _const_spec((D_MODEL, D_MODEL)),
            pl.BlockSpec((MEM_LEN, D_MODEL), lambda b, i: (b, 0)),
            _const_spec((1, D_MODEL)),
            _const_spec((D_MODEL, D_MODEL)),
            _const_spec((D_MODEL, D_MODEL)),
            _const_spec((D_MODEL, D_MODEL)),
        ],
        out_specs=tok(D_MODEL),
        out_shape=jax.ShapeDtypeStruct((n, D_MODEL), F32),
        scratch_shapes=[pltpu.VMEM((MEM_LEN, D_MODEL), BF16), pltpu.VMEM((MEM_LEN, D_MODEL), BF16)],
        compiler_params=pltpu.CompilerParams(
            dimension_semantics=("arbitrary", "arbitrary"), vmem_limit_bytes=VMEM_LIMIT_BYTES),
        name="mix_xattn",
    )(xf, oh, om, wo, g, wq, memf, gm, wk, wv, wxo)


def _mlp_kernel(x_ref, g_ref, w1_ref, w2_ref, gf_ref, y_ref, *, final_norm):
    x = x_ref[...]
    h = _rms(x, g_ref[...]).astype(BF16)
    ff_chunk = D_MODEL
    acc = x
    for c in range(D_FF // ff_chunk):
        cs = slice(c * ff_chunk, (c + 1) * ff_chunk)
        z = jnp.maximum(_dot(h, w1_ref[:, cs]), 0.0)
        acc = acc + _dot((z * z).astype(BF16), w2_ref[cs, :])
    y_ref[...] = _rms(acc, gf_ref[...]) if final_norm else acc


def _mlp(xf, g, w1, w2, gf, final_norm):
    n = xf.shape[0]
    tm = TOKEN_TILE
    return pl.pallas_call(
        functools.partial(_mlp_kernel, final_norm=final_norm),
        grid=(n // tm,),
        in_specs=[
            pl.BlockSpec((tm, D_MODEL), lambda i: (i, 0)),
            _const_spec((1, D_MODEL)),
            _const_spec((D_MODEL, D_FF)),
            _const_spec((D_FF, D_MODEL)),
            _const_spec((1, D_MODEL)),
        ],
        out_specs=pl.BlockSpec((tm, D_MODEL), lambda i: (i, 0)),
        out_shape=jax.ShapeDtypeStruct((n, D_MODEL), F32),
        compiler_params=pltpu.CompilerParams(
            dimension_semantics=("arbitrary",), vmem_limit_bytes=VMEM_LIMIT_BYTES),
        name="mlp",
    )(xf, g, w1, w2, gf)


def _rotary_tables(seq):
    half = ROT_DIM // 2
    inv_freq = np.float32(ROPE_THETA) ** (-np.arange(half, dtype=np.float32) * np.float32(2.0 / ROT_DIM))
    ang = (np.arange(seq, dtype=np.float32)[:, None] * inv_freq[None, :]).astype(np.float64)
    cos, sin = np.cos(ang), np.sin(ang)
    rest = MOBA_HEAD_DIM - ROT_DIM
    cos_t = np.concatenate([cos, cos, np.ones((seq, rest))], axis=-1)
    sa_t = np.concatenate([-sin, np.zeros((seq, MOBA_HEAD_DIM - half))], axis=-1)
    sb_t = np.concatenate([np.zeros((seq, half)), sin, np.zeros((seq, rest))], axis=-1)
    return tuple(jnp.asarray(t, F32) for t in (cos_t, sa_t, sb_t))


def _cumsum_matrix():
    T, C = HGRN_TILE, HGRN_CHUNK
    r = np.arange(T)
    same = (r[:, None] // C) == (r[None, :] // C)
    return jnp.asarray(same & (r[None, :] <= r[:, None]), BF16)


def kernel(x, mem, norm_mix, w_in, lb_logits, hgrn_norm, w_out, norm_xattn, norm_mem,
           w_xq, w_xk, w_xv, w_xo, norm_mlp, w_ff1, w_ff2, norm_final):
    batch, seq, d_model = x.shape
    assert d_model == D_MODEL and mem.shape == (batch, MEM_LEN, D_MODEL), (x.shape, mem.shape)
    assert w_in.shape[1:] == (D_MODEL, IN_COLS) and w_ff1.shape[1:] == (D_MODEL, D_FF), (w_in.shape, w_ff1.shape)
    assert seq % TOKEN_TILE == 0 and seq % (HGRN_TILE * HGRN_SUBTILES) == 0, seq
    assert seq % (MOBA_BLOCK * MOBA_QUERY_BLOCKS) == 0 and MOBA_QUERY_BLOCKS == 2, seq
    n = batch * seq
    xf = x.reshape(n, D_MODEL)
    lb_table = jnp.cumsum(jax.nn.softmax(lb_logits.astype(F32), axis=0), axis=0)
    cos_t, sa_t, sb_t = _rotary_tables(seq)
    cum = _cumsum_matrix()
    row = lambda v: v.reshape(1, -1)
    depth = norm_mix.shape[0]
    for l in range(depth):
        qt, kt, ke, dec, hv, gate, pm, vt = _inproj(
            xf, row(norm_mix[l]), w_in[l].astype(BF16), cos_t, sa_t, sb_t,
            row(lb_table[l]), row(hgrn_norm[l]), cum, batch, seq)
        o_hgrn, (wo, wq, wk, wv, wxo, w1, w2) = _hgrn(
            qt, kt, ke, dec, hv, gate,
            (w_out[l], w_xq[l], w_xk[l], w_xv[l], w_xo[l], w_ff1[l], w_ff2[l]), batch, seq)
        o_moba = _moba(pm, vt, batch, seq)
        x2 = _mix_xattn(xf, o_hgrn, o_moba, wo, row(norm_xattn[l]), wq,
                        mem.reshape(-1, D_MODEL), row(norm_mem[l]), wk, wv, wxo, batch, seq)
        xf = _mlp(x2, row(norm_mlp[l]), w1, w2, row(norm_final), final_norm=(l == depth - 1))
    return xf.reshape(batch, seq, D_MODEL)
```
